```python
import math
import jax, jax.numpy as jnp
from jax import lax
import numpy as np

D_MODEL = 4096
BATCH = 4
SEQ = 2048
DEPTH = 2

HEAD_DIM = 128
MIX_WIDTH = D_MODEL
MOBA_HEADS = (MIX_WIDTH // 2) // HEAD_DIM
MOBA_WIDTH = MOBA_HEADS * HEAD_DIM
MOBA_BLOCK = 256
MOBA_TOPK = 3
MOBA_Q_CHUNK = 64
DIFF_QK_DIM = 128
DIFF_V_DIM = 2 * DIFF_QK_DIM
DIFF_HEADS = (MIX_WIDTH // 2) // DIFF_V_DIM
DIFF_QK_WIDTH = DIFF_HEADS * 2 * DIFF_QK_DIM
DIFF_V_WIDTH = DIFF_HEADS * DIFF_V_DIM
DIFF_Q_BLOCK = 128
IN_WIDTH = 3 * MOBA_WIDTH + 2 * DIFF_QK_WIDTH + DIFF_V_WIDTH + 2 * D_MODEL
D_FF = 4 * D_MODEL
EPS = 1e-6
NEG = -1e30

kernel_name = "hybrid_moba_diffattn_gated_block"


def rms_norm(x, g):
    xf = x.astype(jnp.float32)
    y = xf * lax.rsqrt(jnp.mean(xf * xf, axis=-1, keepdims=True) + EPS)
    return (y * g.astype(jnp.float32)).astype(x.dtype)


def alibi_slopes(n_heads):
    return 2.0 ** (-8.0 * jnp.arange(1, n_heads + 1, dtype=jnp.float32) / n_heads)


def moba_attention(q, k, v):
    B, H, S, Dh = q.shape
    nb = -(-S // MOBA_BLOCK)
    s_pad = nb * MOBA_BLOCK
    pad = s_pad - S
    kp = jnp.pad(k, ((0, 0), (0, 0), (0, pad), (0, 0)))
    vp = jnp.pad(v, ((0, 0), (0, 0), (0, pad), (0, 0)))
    kb = kp.reshape(B, H, nb, MOBA_BLOCK, Dh)
    vb = vp.reshape(B, H, nb, MOBA_BLOCK, Dh)
    k_mean = jnp.mean(kb.astype(jnp.float32), axis=3)
    topk = min(MOBA_TOPK, nb)
    scale = Dh ** -0.5
    slopes = alibi_slopes(H)
    sl5 = slopes[None, :, None, None, None]
    sl4 = slopes[None, :, None, None]
    bi = jnp.arange(B)[:, None, None, None]
    hi = jnp.arange(H)[None, :, None, None]
    n_chunks = S // MOBA_Q_CHUNK
    blk_ar = jnp.arange(MOBA_BLOCK)

    def chunk(c):
        t0 = c * MOBA_Q_CHUNK
        qc = lax.dynamic_slice_in_dim(q, t0, MOBA_Q_CHUNK, axis=2)
        pos_q = t0 + jnp.arange(MOBA_Q_CHUNK)
        own = t0 // MOBA_BLOCK
        gate = jnp.einsum('bhcd,bhnd->bhcn', qc.astype(jnp.float32), k_mean)
        past = jnp.arange(nb) < own
        gate = jnp.where(past, gate, NEG)
        _, idx = lax.top_k(gate, topk)
        valid = idx < own
        k_sel = kb[bi, hi, idx]
        v_sel = vb[bi, hi, idx]
        s_sel = jnp.einsum('bhcd,bhcjsd->bhcjs', qc, k_sel).astype(jnp.float32) * scale
        pos_sel = idx[..., None] * MOBA_BLOCK + blk_ar
        s_sel = s_sel - sl5 * (pos_q[:, None, None] - pos_sel).astype(jnp.float32)
        s_sel = jnp.where(valid[..., None], s_sel, NEG)
        k_own = lax.dynamic_slice_in_dim(kp, own * MOBA_BLOCK, MOBA_BLOCK, axis=2)
        v_own = lax.dynamic_slice_in_dim(vp, own * MOBA_BLOCK, MOBA_BLOCK, axis=2)
        s_own = jnp.einsum('bhcd,bhsd->bhcs', qc, k_own).astype(jnp.float32) * scale
        dist = (pos_q[:, None] - (own * MOBA_BLOCK + blk_ar)[None, :]).astype(jnp.float32)
        s_own = jnp.where(dist >= 0, s_own - sl4 * dist, NEG)
        scores = jnp.concatenate(
            [s_sel.reshape(B, H, MOBA_Q_CHUNK, topk * MOBA_BLOCK), s_own], axis=-1)
        p = jax.nn.softmax(scores, axis=-1)
        p_sel = p[..., :topk * MOBA_BLOCK].reshape(B, H, MOBA_Q_CHUNK, topk, MOBA_BLOCK).astype(v.dtype)
        p_own = p[..., topk * MOBA_BLOCK:].astype(v.dtype)
        return (jnp.einsum('bhcjs,bhcjsd->bhcd', p_sel, v_sel)
                + jnp.einsum('bhcs,bhsd->bhcd', p_own, v_own))

    outs = lax.map(chunk, jnp.arange(n_chunks))
    return outs.transpose(1, 2, 0, 3, 4).reshape(B, H, S, Dh)


def diff_attention(q, k, v, lam, lam_init, g_subln):
    B, H, _, S, dq = q.shape
    dv = v.shape[-1]
    scale = dq ** -0.5
    sl = alibi_slopes(H)[None, :, None, None, None]
    pos_k = jnp.arange(S)
    n_blocks = S // DIFF_Q_BLOCK

    def block(i):
        t0 = i * DIFF_Q_BLOCK
        qb = lax.dynamic_slice_in_dim(q, t0, DIFF_Q_BLOCK, axis=3)
        s = jnp.einsum('bhmqd,bhmsd->bhmqs', qb, k).astype(jnp.float32) * scale
        dist = ((t0 + jnp.arange(DIFF_Q_BLOCK))[:, None] - pos_k[None, :]).astype(jnp.float32)
        s = jnp.where(dist >= 0, s - sl * dist, NEG)
        p = jax.nn.softmax(s, axis=-1)
        a = p[:, :, 0] - lam * p[:, :, 1]
        return jnp.einsum('bhqs,bhsd->bhqd', a.astype(v.dtype), v)

    o = lax.map(block, jnp.arange(n_blocks))
    o = o.transpose(1, 2, 0, 3, 4).reshape(B, H, S, dv)
    return rms_norm(o, g_subln) * (1.0 - lam_init)


def hybrid_layer(x, layer_idx, w_in, w_proj_a, w_proj_b, w_out, w_up, w_down,
                 g_pre_mix, g_post_mix, g_pre_mlp, g_post_mlp, g_subln,
                 lam_q1, lam_k1, lam_q2, lam_k2):
    B, S, _ = x.shape
    h = rms_norm(x, g_pre_mix)
    proj = jnp.einsum('bsd,de->bse', h, w_in)
    offs = np.cumsum([MOBA_WIDTH, MOBA_WIDTH, MOBA_WIDTH, DIFF_QK_WIDTH,
                      DIFF_QK_WIDTH, DIFF_V_WIDTH, D_MODEL]).tolist()
    qa, ka, va, qb, kb, vb, ga, gb = jnp.split(proj, offs, axis=-1)

    def heads(t, n, d):
        return t.reshape(B, S, n, d).transpose(0, 2, 1, 3)

    o_a = moba_attention(heads(qa, MOBA_HEADS, HEAD_DIM), heads(ka, MOBA_HEADS, HEAD_DIM),
                         heads(va, MOBA_HEADS, HEAD_DIM))
    o_a = o_a.transpose(0, 2, 1, 3).reshape(B, S, MOBA_WIDTH)

    lam_init = 0.8 - 0.6 * math.exp(-0.3 * layer_idx)
    lam = (jnp.exp(jnp.sum(lam_q1.astype(jnp.float32) * lam_k1.astype(jnp.float32)))
           - jnp.exp(jnp.sum(lam_q2.astype(jnp.float32) * lam_k2.astype(jnp.float32)))
           + lam_init)
    qd = qb.reshape(B, S, DIFF_HEADS, 2, DIFF_QK_DIM).transpose(0, 2, 3, 1, 4)
    kd = kb.reshape(B, S, DIFF_HEADS, 2, DIFF_QK_DIM).transpose(0, 2, 3, 1, 4)
    vd = heads(vb, DIFF_HEADS, DIFF_V_DIM)
    o_b = diff_attention(qd, kd, vd, lam, lam_init, g_subln)
    o_b = o_b.transpose(0, 2, 1, 3).reshape(B, S, DIFF_V_WIDTH)

    y = (jax.nn.sigmoid(ga) * jnp.einsum('bse,ed->bsd', o_a, w_proj_a)
         + jax.nn.sigmoid(gb) * jnp.einsum('bse,ed->bsd', o_b, w_proj_b))
    mix = jnp.einsum('bsd,de->bse', y, w_out)
    x = x + rms_norm(mix, g_post_mix)
    h = rms_norm(x, g_pre_mlp)
    u = jax.nn.relu(jnp.einsum('bsd,df->bsf', h, w_up))
    m = jnp.einsum('bsf,fd->bsd', u * u, w_down)
    return x + rms_norm(m, g_post_mlp)


def setup_inputs(seed: int = 0) -> dict:
    key = jax.random.key(seed)
    ks = jax.random.split(key, 17)
    f32 = jnp.float32

    def nrm(k, shape, fan_in):
        return jax.random.normal(k, shape, f32) * (fan_in ** -0.5)

    def gain(k, shape):
        return 1.0 + 0.05 * jax.random.normal(k, shape, f32)

    return {
        "x": jax.random.normal(ks[0], (BATCH, SEQ, D_MODEL), f32),
        "w_in": nrm(ks[1], (DEPTH, D_MODEL, IN_WIDTH), D_MODEL),
        "w_proj_a": nrm(ks[2], (DEPTH, MOBA_WIDTH, D_MODEL), MOBA_WIDTH),
        "w_proj_b": nrm(ks[3], (DEPTH, DIFF_V_WIDTH, D_MODEL), DIFF_V_WIDTH),
        "w_out": nrm(ks[4], (DEPTH, D_MODEL, D_MODEL), D_MODEL),
        "w_up": nrm(ks[5], (DEPTH, D_MODEL, D_FF), D_MODEL),
        "w_down": nrm(ks[6], (DEPTH, D_FF, D_MODEL), D_FF),
        "g_pre_mix": gain(ks[7], (DEPTH, D_MODEL)),
        "g_post_mix": gain(ks[8], (DEPTH, D_MODEL)),
        "g_pre_mlp": gain(ks[9], (DEPTH, D_MODEL)),
        "g_post_mlp": gain(ks[10], (DEPTH, D_MODEL)),
        "g_subln": gain(ks[11], (DEPTH, DIFF_V_DIM)),
        "lam_q1": 0.1 * jax.random.normal(ks[12], (DEPTH, DIFF_QK_DIM), f32),
        "lam_k1": 0.1 * jax.random.normal(ks[13], (DEPTH, DIFF_QK_DIM), f32),
        "lam_q2": 0.1 * jax.random.normal(ks[14], (DEPTH, DIFF_QK_DIM), f32),
        "lam_k2": 0.1 * jax.random.normal(ks[15], (DEPTH, DIFF_QK_DIM), f32),
    }


def reference(x, w_in, w_proj_a, w_proj_b, w_out, w_up, w_down,
              g_pre_mix, g_post_mix, g_pre_mlp, g_post_mlp, g_subln,
              lam_q1, lam_k1, lam_q2, lam_k2):
    for l in range(DEPTH):
        x = hybrid_layer(x, l, w_in[l], w_proj_a[l], w_proj_b[l], w_out[l], w_up[l], w_down[l],
                         g_pre_mix[l], g_post_mix[l], g_pre_mlp[l], g_post_mlp[l], g_subln[l],
                         lam_q1[l], lam_k1[l], lam_q2[l], lam_k2[l])
    return x
```

```python
import functools
import math

import jax
import jax.numpy as jnp
from jax import lax
from jax.experimental import pallas as pl
from jax.experimental.pallas import tpu as pltpu

F32 = jnp.float32
BF16 = jnp.bfloat16

V7X_LANES = 128
V7X_VMEM_BYTES = 64 * 2**20
V7X_VMEM_COMPILER_RESERVE = 12 * 2**20

HEAD_DIM = 128
DIFF_V_DIM = 256
ATTN_BLOCK = 256
MOBA_TOPK = 3
EPS = 1e-6
NEG = -1e30

_NT_DIMS = (((1,), (1,)), ((), ()))


def _vmem_limit(block_bytes, scratch_bytes=0):
    want = 2 * block_bytes + scratch_bytes + V7X_VMEM_COMPILER_RESERVE
    return int(min(want, V7X_VMEM_BYTES - 2 * 2**20))


def _nbytes(shape, dtype):
    return math.prod(shape) * jnp.dtype(dtype).itemsize


def _rms(x):
    return x * lax.rsqrt(jnp.mean(x * x, axis=-1, keepdims=True) + EPS)


def _rmsnorm_cast_kernel(x_ref, g_ref, h_ref):
    h_ref[...] = (_rms(x_ref[...]) * g_ref[...]).astype(h_ref.dtype)


def _rmsnorm_cast(x, g, *, rows=256):
    t, d = x.shape
    assert t % rows == 0
    blocks = _nbytes((rows, d), F32) + _nbytes((rows, d), BF16)
    return pl.pallas_call(
        _rmsnorm_cast_kernel,
        grid=(t // rows,),
        in_specs=[pl.BlockSpec((rows, d), lambda i: (i, 0)),
                  pl.BlockSpec((1, d), lambda i: (0, 0))],
        out_specs=pl.BlockSpec((rows, d), lambda i: (i, 0)),
        out_shape=jax.ShapeDtypeStruct((t, d), BF16),
        compiler_params=pltpu.CompilerParams(
            dimension_semantics=("parallel",), vmem_limit_bytes=_vmem_limit(blocks)),
        name="rmsnorm_cast",
    )(x, g.reshape(1, d))


def _residual_kernel(x_ref, m_ref, gpost_ref, gnext_ref, xo_ref, h_ref):
    xn = x_ref[...] + _rms(m_ref[...]) * gpost_ref[...]
    xo_ref[...] = xn
    h_ref[...] = (_rms(xn) * gnext_ref[...]).astype(h_ref.dtype)


def _residual_last_kernel(x_ref, m_ref, gpost_ref, xo_ref):
    xo_ref[...] = x_ref[...] + _rms(m_ref[...]) * gpost_ref[...]


def _residual(x, m, g_post, g_next, *, rows=256):
    t, d = x.shape
    assert t % rows == 0
    row_spec = pl.BlockSpec((rows, d), lambda i: (i, 0))
    gain_spec = pl.BlockSpec((1, d), lambda i: (0, 0))
    blocks = 3 * _nbytes((rows, d), F32) + _nbytes((rows, d), BF16)
    params = pltpu.CompilerParams(
        dimension_semantics=("parallel",), vmem_limit_bytes=_vmem_limit(blocks))
    if g_next is None:
        return pl.pallas_call(
            _residual_last_kernel,
            grid=(t // rows,),
            in_specs=[row_spec, row_spec, gain_spec],
            out_specs=row_spec,
            out_shape=jax.ShapeDtypeStruct((t, d), F32),
            compiler_params=params,
            name="residual_last",
        )(x, m, g_post.reshape(1, d)), None
    return pl.pallas_call(
        _residual_kernel,
        grid=(t // rows,),
        in_specs=[row_spec, row_spec, gain_spec, gain_spec],
        out_specs=[row_spec, row_spec],
        out_shape=[jax.ShapeDtypeStruct((t, d), F32), jax.ShapeDtypeStruct((t, d), BF16)],
        compiler_params=params,
        name="residual_norm",
    )(x, m, g_post.reshape(1, d), g_next.reshape(1, d))


def _identity(x):
    return x


def _relu_squared(x):
    r = jnp.maximum(x, 0.0)
    return r * r


def _matmul_kernel(x_ref, w_ref, o_ref, *scratch, n_k, epilogue):
    part = jnp.dot(x_ref[...], w_ref[...], preferred_element_type=F32)
    if n_k == 1:
        o_ref[...] = epilogue(part).astype(o_ref.dtype)
        return
    (acc_ref,) = scratch
    k = pl.program_id(2)

    @pl.when(k == 0)
    def _():
        acc_ref[...] = jnp.zeros_like(acc_ref)

    acc_ref[...] += part

    @pl.when(k == n_k - 1)
    def _():
        o_ref[...] = epilogue(acc_ref[...]).astype(o_ref.dtype)


def _matmul(x, w, *, out_dtype, tm, tn, tk=None, epilogue=_identity, name):
    m, k = x.shape
    _, n = w.shape
    tk = k if tk is None else tk
    assert m % tm == 0 and n % tn == 0 and k % tk == 0
    n_m, n_n, n_k = m // tm, n // tn, k // tk
    x_reread = (n_n - 1) * m * k
    w_reread = (n_m - 1) * k * n
    if x_reread <= w_reread:
        grid = (n_n, n_m, n_k)
        x_map = lambda j, i, kk: (i, kk)
        w_map = lambda j, i, kk: (kk, j)
        o_map = lambda j, i, kk: (i, j)
    else:
        grid = (n_m, n_n, n_k)
        x_map = lambda i, j, kk: (i, kk)
        w_map = lambda i, j, kk: (kk, j)
        o_map = lambda i, j, kk: (i, j)
    blocks = _nbytes((tm, tk), x.dtype) + _nbytes((tk, tn), w.dtype) + _nbytes((tm, tn), out_dtype)
    scratch = [pltpu.VMEM((tm, tn), F32)] if n_k > 1 else []
    return pl.pallas_call(
        functools.partial(_matmul_kernel, n_k=n_k, epilogue=epilogue),
        grid=grid,
        in_specs=[pl.BlockSpec((tm, tk), x_map), pl.BlockSpec((tk, tn), w_map)],
        out_specs=pl.BlockSpec((tm, tn), o_map),
        out_shape=jax.ShapeDtypeStruct((m, n), out_dtype),
        scratch_shapes=scratch,
        compiler_params=pltpu.CompilerParams(
            dimension_semantics=("parallel", "parallel", "arbitrary"),
            vmem_limit_bytes=_vmem_limit(blocks, _nbytes((tm, tn), F32) * len(scratch))),
        name=name,
    )(x, w)


def _value_t_kernel(wt_ref, x_ref, o_ref, *, n_chunks):
    r = lax.dot_general(wt_ref[...], x_ref[...], _NT_DIMS, preferred_element_type=F32)
    for c in range(n_chunks):
        o_ref[c] = r[:, c * ATTN_BLOCK:(c + 1) * ATTN_BLOCK].astype(o_ref.dtype)


def _value_t(wt, x, *, tf, tm):
    f, k = wt.shape
    t, _ = x.shape
    assert f % tf == 0 and t % tm == 0 and tm % ATTN_BLOCK == 0
    n_chunks = tm // ATTN_BLOCK
    blocks = _nbytes((tf, k), BF16) + _nbytes((tm, k), BF16) + _nbytes((tf, tm), BF16)
    return pl.pallas_call(
        functools.partial(_value_t_kernel, n_chunks=n_chunks),
        grid=(f // tf, t // tm),
        in_specs=[pl.BlockSpec((tf, k), lambda j, i: (j, 0)),
                  pl.BlockSpec((tm, k), lambda j, i: (i, 0))],
        out_specs=pl.BlockSpec((n_chunks, tf, ATTN_BLOCK), lambda j, i: (i, j, 0)),
        out_shape=jax.ShapeDtypeStruct((t // ATTN_BLOCK, f, ATTN_BLOCK), BF16),
        compiler_params=pltpu.CompilerParams(
            dimension_semantics=("parallel", "parallel"),
            vmem_limit_bytes=_vmem_limit(blocks)),
        name="value_t_proj",
    )(wt, x)


def _gated_proj_kernel(oa_ref, ob_ref, wa_ref, wb_ref, ga_ref, gb_ref, y_ref):
    pa = jnp.dot(oa_ref[...], wa_ref[...], preferred_element_type=F32)
    pb = jnp.dot(ob_ref[...], wb_ref[...], preferred_element_type=F32)
    y = jax.nn.sigmoid(ga_ref[...]) * pa + jax.nn.sigmoid(gb_ref[...]) * pb
    y_ref[...] = y.astype(y_ref.dtype)


def _gated_proj(o_a, o_b, w_a, w_b, gates, *, tm, tn):
    t, ka = o_a.shape
    _, kb = o_b.shape
    d = w_a.shape[1]
    assert t % tm == 0 and d % tn == 0
    n_n = d // tn
    blocks = (_nbytes((tm, ka), BF16) + _nbytes((tm, kb), BF16) + _nbytes((ka, tn), BF16)
              + _nbytes((kb, tn), BF16) + 2 * _nbytes((tm, tn), F32) + _nbytes((tm, tn), BF16))
    return pl.pallas_call(
        _gated_proj_kernel,
        grid=(n_n, t // tm),
        in_specs=[pl.BlockSpec((tm, ka), lambda j, i: (i, 0)),
                  pl.BlockSpec((tm, kb), lambda j, i: (i, 0)),
                  pl.BlockSpec((ka, tn), lambda j, i: (0, j)),
                  pl.BlockSpec((kb, tn), lambda j, i: (0, j)),
                  pl.BlockSpec((tm, tn), lambda j, i: (i, j)),
                  pl.BlockSpec((tm, tn), lambda j, i: (i, n_n + j))],
        out_specs=pl.BlockSpec((tm, tn), lambda j, i: (i, j)),
        out_shape=jax.ShapeDtypeStruct((t, d), BF16),
        compiler_params=pltpu.CompilerParams(
            dimension_semantics=("parallel", "parallel"),
            vmem_limit_bytes=_vmem_limit(blocks)),
        name="gated_proj",
    )(o_a, o_b, w_a, w_b, gates, gates)


def _alibi_tile(slope):
    kk = lax.broadcasted_iota(jnp.int32, (ATTN_BLOCK, ATTN_BLOCK), 0)
    qq = lax.broadcasted_iota(jnp.int32, (ATTN_BLOCK, ATTN_BLOCK), 1)
    return kk <= qq, slope * (kk - qq).astype(F32)


def _moba_kernel(slopes_ref, q_ref, k_ref, vt_ref, o_ref, kmean_ref, bias_ref, *, scale):
    n_blocks = kmean_ref.shape[0]
    h = pl.program_id(1)
    i = pl.program_id(2)

    @pl.when(i == 0)
    def _():
        for j in range(n_blocks):
            kj = k_ref[j * ATTN_BLOCK:(j + 1) * ATTN_BLOCK, :].astype(F32)
            kmean_ref[j:j + 1, :] = jnp.sum(kj, axis=0, keepdims=True) * (1.0 / ATTN_BLOCK)

    q = q_ref[...]
    slope = slopes_ref[h]

    gate = lax.dot_general(kmean_ref[...], q.astype(F32), _NT_DIMS,
                           precision=lax.Precision.HIGHEST, preferred_element_type=F32)
    row = lax.broadcasted_iota(jnp.int32, gate.shape, 0)
    rank = jnp.zeros(gate.shape, jnp.int32)
    for jp in range(n_blocks):
        g_jp = gate[jp:jp + 1, :]
        beats = (g_jp > gate) | ((g_jp == gate) & (jp < row))
        rank = rank + jnp.where(beats, 1, 0) * jnp.where(jp < i, 1, 0)
    selected = (row < i) & (rank < MOBA_TOPK)
    block_off = slope * ((row - i) * ATTN_BLOCK).astype(F32)
    bias_ref[...] = jnp.where(selected, block_off, NEG)

    causal, alibi = _alibi_tile(slope)

    k_own = k_ref[pl.ds(pl.multiple_of(i * ATTN_BLOCK, ATTN_BLOCK), ATTN_BLOCK), :]
    s = lax.dot_general(k_own, q, _NT_DIMS, preferred_element_type=F32) * scale
    s = jnp.where(causal, s + alibi, NEG)
    m0 = jnp.max(s, axis=0, keepdims=True)
    p = jnp.exp(s - m0)
    l0 = jnp.sum(p, axis=0, keepdims=True)
    acc0 = jnp.dot(vt_ref[i], p.astype(BF16), preferred_element_type=F32)

    def body(j, carry):
        m, l, acc = carry
        kj = k_ref[pl.ds(pl.multiple_of(j * ATTN_BLOCK, ATTN_BLOCK), ATTN_BLOCK), :]
        s = lax.dot_general(kj, q, _NT_DIMS, preferred_element_type=F32) * scale
        s = s + alibi + bias_ref[pl.ds(j, 1), :]
        m_new = jnp.maximum(m, jnp.max(s, axis=0, keepdims=True))
        alpha = jnp.exp(m - m_new)
        p = jnp.exp(s - m_new)
        l = alpha * l + jnp.sum(p, axis=0, keepdims=True)
        acc = alpha * acc + jnp.dot(vt_ref[j], p.astype(BF16), preferred_element_type=F32)
        return m_new, l, acc

    _, l, acc = lax.fori_loop(0, i, body, (m0, l0, acc0))
    o_ref[...] = (acc * (1.0 / l)).T.astype(o_ref.dtype)


def _moba_attention(qk, vt, slopes, *, batch, seq, heads, q_col, k_col, v_row):
    nb = seq // ATTN_BLOCK
    t = batch * seq
    blocks = (_nbytes((ATTN_BLOCK, HEAD_DIM), BF16) * 2 + _nbytes((seq, HEAD_DIM), BF16)
              + _nbytes((nb, HEAD_DIM, ATTN_BLOCK), BF16))
    return pl.pallas_call(
        functools.partial(_moba_kernel, scale=HEAD_DIM ** -0.5),
        grid=(batch, heads, nb),
        in_specs=[pl.BlockSpec(memory_space=pltpu.SMEM),
                  pl.BlockSpec((ATTN_BLOCK, HEAD_DIM), lambda b, h, i: (b * nb + i, q_col + h)),
                  pl.BlockSpec((seq, HEAD_DIM), lambda b, h, i: (b, k_col + h)),
                  pl.BlockSpec((nb, HEAD_DIM, ATTN_BLOCK), lambda b, h, i: (b, v_row + h, 0))],
        out_specs=pl.BlockSpec((ATTN_BLOCK, HEAD_DIM), lambda b, h, i: (b * nb + i, h)),
        out_shape=jax.ShapeDtypeStruct((t, heads * HEAD_DIM), BF16),
        scratch_shapes=[pltpu.VMEM((nb, HEAD_DIM), F32), pltpu.VMEM((nb, ATTN_BLOCK), F32)],
        compiler_params=pltpu.CompilerParams(
            dimension_semantics=("parallel", "parallel", "arbitrary"),
            vmem_limit_bytes=_vmem_limit(blocks)),
        name="moba_attention",
    )(slopes, qk, qk, vt)


def _diff_kernel(slopes_ref, lamv_ref, g_ref, q1_ref, q2_ref, k1_ref, k2_ref, vt_ref, o_ref,
                 acc_ref, *, scale, lam_init):
    h = pl.program_id(1)
    i = pl.program_id(2)
    slope = slopes_ref[h]
    causal, alibi = _alibi_tile(slope)
    qs = (q1_ref[...], q2_ref[...])
    k_refs = (k1_ref, k2_ref)

    def scores(mi, j):
        kj = k_refs[mi][pl.ds(pl.multiple_of(j * ATTN_BLOCK, ATTN_BLOCK), ATTN_BLOCK), :]
        return lax.dot_general(kj, qs[mi], _NT_DIMS, preferred_element_type=F32) * scale

    stats = []
    v_own = vt_ref[i]
    for mi in range(2):
        s = jnp.where(causal, scores(mi, i) + alibi, NEG)
        m0 = jnp.max(s, axis=0, keepdims=True)
        p = jnp.exp(s - m0)
        stats += [m0, jnp.sum(p, axis=0, keepdims=True)]
        acc_ref[mi] = jnp.dot(v_own, p.astype(BF16), preferred_element_type=F32)

    def body(j, carry):
        out = []
        vj = vt_ref[j]
        bias = alibi + slope * ((j - i) * ATTN_BLOCK).astype(F32)
        for mi in range(2):
            m, l = carry[2 * mi], carry[2 * mi + 1]
            s = scores(mi, j) + bias
            m_new = jnp.maximum(m, jnp.max(s, axis=0, keepdims=True))
            alpha = jnp.exp(m - m_new)
            p = jnp.exp(s - m_new)
            out += [m_new, alpha * l + jnp.sum(p, axis=0, keepdims=True)]
            acc_ref[mi] = alpha * acc_ref[mi] + jnp.dot(vj, p.astype(BF16),
                                                        preferred_element_type=F32)
        return tuple(out)

    _, l1, _, l2 = lax.fori_loop(0, i, body, tuple(stats))

    lv = lamv_ref[...]
    lam = (jnp.exp(jnp.sum(lv[0:1] * lv[1:2], axis=-1, keepdims=True))
           - jnp.exp(jnp.sum(lv[2:3] * lv[3:4], axis=-1, keepdims=True)) + lam_init)
    o = acc_ref[0] * (1.0 / l1) - lam * (acc_ref[1] * (1.0 / l2))
    o = o * lax.rsqrt(jnp.mean(o * o, axis=0, keepdims=True) + EPS)
    o_ref[...] = (o.T * g_ref[...] * (1.0 - lam_init)).astype(o_ref.dtype)


def _diff_attention(qk, vt, slopes, lam_vecs, g_subln, *, batch, seq, heads, q_col, k_col, v_row,
                    lam_init):
    nb = seq // ATTN_BLOCK
    t = batch * seq
    blocks = (2 * _nbytes((ATTN_BLOCK, HEAD_DIM), BF16) + 2 * _nbytes((seq, HEAD_DIM), BF16)
              + _nbytes((nb, DIFF_V_DIM, ATTN_BLOCK), BF16) + _nbytes((ATTN_BLOCK, DIFF_V_DIM), BF16))
    q_spec = lambda mi: pl.BlockSpec((ATTN_BLOCK, HEAD_DIM),
                                     lambda b, h, i: (b * nb + i, q_col + 2 * h + mi))
    k_spec = lambda mi: pl.BlockSpec((seq, HEAD_DIM), lambda b, h, i: (b, k_col + 2 * h + mi))
    return pl.pallas_call(
        functools.partial(_diff_kernel, scale=HEAD_DIM ** -0.5, lam_init=lam_init),
        grid=(batch, heads, nb),
        in_specs=[pl.BlockSpec(memory_space=pltpu.SMEM),
                  pl.BlockSpec((4, HEAD_DIM), lambda b, h, i: (0, 0)),
                  pl.BlockSpec((1, DIFF_V_DIM), lambda b, h, i: (0, 0)),
                  q_spec(0), q_spec(1), k_spec(0), k_spec(1),
                  pl.BlockSpec((nb, DIFF_V_DIM, ATTN_BLOCK), lambda b, h, i: (b, v_row + h, 0))],
        out_specs=pl.BlockSpec((ATTN_BLOCK, DIFF_V_DIM), lambda b, h, i: (b * nb + i, h)),
        out_shape=jax.ShapeDtypeStruct((t, heads * DIFF_V_DIM), BF16),
        scratch_shapes=[pltpu.VMEM((2, DIFF_V_DIM, ATTN_BLOCK), F32)],
        compiler_params=pltpu.CompilerParams(
            dimension_semantics=("parallel", "parallel", "arbitrary"),
            vmem_limit_bytes=_vmem_limit(blocks, _nbytes((2, DIFF_V_DIM, ATTN_BLOCK), F32))),
        name="diff_attention",
    )(slopes, lam_vecs, g_subln.reshape(1, DIFF_V_DIM), qk, qk, qk, qk, vt)


def _alibi_slopes(n_heads):
    return 2.0 ** (-8.0 * jnp.arange(1, n_heads + 1, dtype=F32) / n_heads)


def _tile(n, want):
    if n <= want:
        return n
    t = want
    while n % t:
        t -= ATTN_BLOCK
    assert t > 0
    return t


def kernel(x, w_in, w_proj_a, w_proj_b, w_out, w_up, w_down, g_pre_mix, g_post_mix, g_pre_mlp,
           g_post_mlp, g_subln, lam_q1, lam_k1, lam_q2, lam_k2):
    batch, seq, d = x.shape
    depth = w_in.shape[0]
    t = batch * seq
    mw = d // 2
    heads_a = mw // HEAD_DIM
    heads_b = mw // DIFF_V_DIM
    assert seq % ATTN_BLOCK == 0 and mw % DIFF_V_DIM == 0
    assert w_in.shape[2] == 6 * mw + 2 * d
    c_va, c_qb, c_vb, c_g = 2 * mw, 3 * mw, 5 * mw, 6 * mw

    slopes_a = _alibi_slopes(heads_a)
    slopes_b = _alibi_slopes(heads_b)
    tm = _tile(t, 1024)

    xf = x.reshape(t, d)
    h = _rmsnorm_cast(xf, g_pre_mix[0])
    for l in range(depth):
        wl = w_in[l]
        w_qk = jnp.concatenate([wl[:, :c_va], wl[:, c_qb:c_vb]], axis=1).astype(BF16)
        w_vt = jnp.concatenate([wl[:, c_va:c_qb], wl[:, c_vb:c_g]], axis=1).T.astype(BF16)
        w_g = wl[:, c_g:].astype(BF16)

        qk = _matmul(h, w_qk, out_dtype=BF16, tm=tm, tn=_tile(4 * mw, 1024), name="qk_proj")
        vt = _value_t(w_vt, h, tf=_tile(2 * mw, 1024), tm=tm)
        gates = _matmul(h, w_g, out_dtype=F32, tm=tm, tn=_tile(2 * d, 1024), name="gate_proj")

        o_a = _moba_attention(qk, vt, slopes_a, batch=batch, seq=seq, heads=heads_a,
                              q_col=0, k_col=heads_a, v_row=0)
        lam_init = 0.8 - 0.6 * math.exp(-0.3 * l)
        lam_vecs = jnp.stack([lam_q1[l], lam_k1[l], lam_q2[l], lam_k2[l]]).astype(F32)
        o_b = _diff_attention(qk, vt, slopes_b, lam_vecs, g_subln[l], batch=batch, seq=seq,
                              heads=heads_b, q_col=2 * heads_a, k_col=2 * heads_a + 2 * heads_b,
                              v_row=mw // DIFF_V_DIM, lam_init=lam_init)

        y = _gated_proj(o_a, o_b, w_proj_a[l].astype(BF16), w_proj_b[l].astype(BF16), gates,
                        tm=_tile(t, 512), tn=_tile(d, 1024))
        mix = _matmul(y, w_out[l].astype(BF16), out_dtype=F32, tm=tm, tn=_tile(d, 1024),
                      name="out_proj")
        xf, h = _residual(xf, mix, g_post_mix[l], g_pre_mlp[l])

        u = _matmul(h, w_up[l].astype(BF16), out_dtype=BF16, tm=tm, tn=_tile(4 * d, 1024),
                    epilogue=_relu_squared, name="mlp_up")
        m = _matmul(u, w_down[l].astype(BF16), out_dtype=F32, tm=tm, tn=_tile(d, 1024),
                    tk=_tile(4 * d, 4096), name="mlp_down")
        g_next = g_pre_mix[l + 1] if l + 1 < depth else None
        xf, h = _residual(xf, m, g_post_mlp[l], g_next)

    return xf.reshape(batch, seq, d)
```

```python
import functools
import math

import jax
import jax.numpy as jnp
from jax import lax
from jax.experimental import pallas as pl
from jax.experimental.pallas import tpu as pltpu

F32 = jnp.float32
BF16 = jnp.bfloat16

V7X_VMEM_BYTES = 64 * 2**20
V7X_VMEM_COMPILER_RESERVE = 12 * 2**20

HEAD_DIM = 128
DIFF_V_DIM = 256
ATTN_BLOCK = 256
MOBA_TOPK = 3
EPS = 1e-6
NEG = -1e30
LOG2E = math.log2(math.e)

MOBA_HEAD_GROUP = 4
DIFF_HEAD_GROUP = 2

_NT_DIMS = (((1,), (1,)), ((), ()))


def _vmem_limit(block_bytes, scratch_bytes=0):
    want = 2 * block_bytes + scratch_bytes + V7X_VMEM_COMPILER_RESERVE
    return int(min(want, V7X_VMEM_BYTES - 2 * 2**20))


def _nbytes(shape, dtype):
    return math.prod(shape) * jnp.dtype(dtype).itemsize


def _rms(x):
    return x * lax.rsqrt(jnp.mean(x * x, axis=-1, keepdims=True) + EPS)


def _rmsnorm_cast_kernel(x_ref, g_ref, h_ref):
    h_ref[...] = (_rms(x_ref[...]) * g_ref[...]).astype(h_ref.dtype)


def _rmsnorm_cast(x, g, *, rows=256):
    t, d = x.shape
    assert t % rows == 0
    blocks = _nbytes((rows, d), F32) + _nbytes((rows, d), BF16)
    return pl.pallas_call(
        _rmsnorm_cast_kernel,
        grid=(t // rows,),
        in_specs=[pl.BlockSpec((rows, d), lambda i: (i, 0)),
                  pl.BlockSpec((1, d), lambda i: (0, 0))],
        out_specs=pl.BlockSpec((rows, d), lambda i: (i, 0)),
        out_shape=jax.ShapeDtypeStruct((t, d), BF16),
        compiler_params=pltpu.CompilerParams(
            dimension_semantics=("parallel",), vmem_limit_bytes=_vmem_limit(blocks)),
        name="rmsnorm_cast",
    )(x, g.reshape(1, d))


def _residual_kernel(x_ref, m_ref, gpost_ref, gnext_ref, xo_ref, h_ref):
    xn = x_ref[...] + _rms(m_ref[...]) * gpost_ref[...]
    xo_ref[...] = xn
    h_ref[...] = (_rms(xn) * gnext_ref[...]).astype(h_ref.dtype)


def _residual_last_kernel(x_ref, m_ref, gpost_ref, xo_ref):
    xo_ref[...] = x_ref[...] + _rms(m_ref[...]) * gpost_ref[...]


def _residual(x, m, g_post, g_next, *, rows=256):
    t, d = x.shape
    assert t % rows == 0
    row_spec = pl.BlockSpec((rows, d), lambda i: (i, 0))
    gain_spec = pl.BlockSpec((1, d), lambda i: (0, 0))
    blocks = 3 * _nbytes((rows, d), F32) + _nbytes((rows, d), BF16)
    params = pltpu.CompilerParams(
        dimension_semantics=("parallel",), vmem_limit_bytes=_vmem_limit(blocks))
    if g_next is None:
        return pl.pallas_call(
            _residual_last_kernel,
            grid=(t // rows,),
            in_specs=[row_spec, row_spec, gain_spec],
            out_specs=row_spec,
            out_shape=jax.ShapeDtypeStruct((t, d), F32),
            compiler_params=params,
            name="residual_last",
        )(x, m, g_post.reshape(1, d)), None
    return pl.pallas_call(
        _residual_kernel,
        grid=(t // rows,),
        in_specs=[row_spec, row_spec, gain_spec, gain_spec],
        out_specs=[row_spec, row_spec],
        out_shape=[jax.ShapeDtypeStruct((t, d), F32), jax.ShapeDtypeStruct((t, d), BF16)],
        compiler_params=params,
        name="residual_norm",
    )(x, m, g_post.reshape(1, d), g_next.reshape(1, d))


def _identity(x):
    return x


def _relu_squared(x):
    r = jnp.maximum(x, 0.0)
    return r * r


def _cast_weight_once(w_ref, wbf_ref, transpose=False):
    @pl.when(pl.program_id(1) == 0)
    def _():
        w = w_ref[...]
        wbf_ref[...] = (w.T if transpose else w).astype(BF16)


def _proj_kernel(x_ref, w_ref, o_ref, wbf_ref, *, epilogue, key_blocked_t):
    _cast_weight_once(w_ref, wbf_ref, transpose=key_blocked_t)
    if key_blocked_t:
        rt = epilogue(lax.dot_general(wbf_ref[...], x_ref[...], _NT_DIMS,
                                      preferred_element_type=F32))
        for c in range(o_ref.shape[0]):
            o_ref[c] = rt[:, c * ATTN_BLOCK:(c + 1) * ATTN_BLOCK].astype(o_ref.dtype)
    else:
        r = epilogue(jnp.dot(x_ref[...], wbf_ref[...], preferred_element_type=F32))
        o_ref[...] = r.astype(o_ref.dtype)


def _proj(x, w, layer, col_tile, n_tiles, *, out_dtype, tm, tn, epilogue=_identity,
          key_blocked_t=False, name):
    m, k = x.shape
    assert m % tm == 0 and w.shape[1] == k and w.shape[2] % tn == 0
    if key_blocked_t:
        assert tm % ATTN_BLOCK == 0
        out_shape = jax.ShapeDtypeStruct((m // ATTN_BLOCK, n_tiles * tn, ATTN_BLOCK), out_dtype)
        out_spec = pl.BlockSpec((tm // ATTN_BLOCK, tn, ATTN_BLOCK), lambda j, i: (i, j, 0))
    else:
        out_shape = jax.ShapeDtypeStruct((m, n_tiles * tn), out_dtype)
        out_spec = pl.BlockSpec((tm, tn), lambda j, i: (i, j))
    blocks = _nbytes((tm, k), BF16) + _nbytes((k, tn), F32) + _nbytes((tm, tn), out_dtype)
    return pl.pallas_call(
        functools.partial(_proj_kernel, epilogue=epilogue, key_blocked_t=key_blocked_t),
        grid=(n_tiles, m // tm),
        in_specs=[pl.BlockSpec((tm, k), lambda j, i: (i, 0)),
                  pl.BlockSpec((None, k, tn), lambda j, i: (layer, 0, col_tile(j)))],
        out_specs=out_spec,
        out_shape=out_shape,
        scratch_shapes=[pltpu.VMEM((tn, k) if key_blocked_t else (k, tn), BF16)],
        compiler_params=pltpu.CompilerParams(
            dimension_semantics=("parallel", "arbitrary"),
            vmem_limit_bytes=_vmem_limit(blocks, _nbytes((k, tn), BF16))),
        name=name,
    )(x, w)


def _matmul_ksplit_kernel(x_ref, w_ref, o_ref, acc_ref, *, n_k):
    k = pl.program_id(2)

    @pl.when(k == 0)
    def _():
        acc_ref[...] = jnp.zeros_like(acc_ref)

    acc_ref[...] += jnp.dot(x_ref[...], w_ref[...], preferred_element_type=F32)

    @pl.when(k == n_k - 1)
    def _():
        o_ref[...] = acc_ref[...].astype(o_ref.dtype)


def _matmul_ksplit(x, w, *, out_dtype, tm, tn, tk, name):
    m, k = x.shape
    _, n = w.shape
    assert m % tm == 0 and n % tn == 0 and k % tk == 0
    n_k = k // tk
    blocks = _nbytes((tm, tk), BF16) + _nbytes((tk, tn), BF16) + _nbytes((tm, tn), out_dtype)
    return pl.pallas_call(
        functools.partial(_matmul_ksplit_kernel, n_k=n_k),
        grid=(m // tm, n // tn, n_k),
        in_specs=[pl.BlockSpec((tm, tk), lambda i, j, kk: (i, kk)),
                  pl.BlockSpec((tk, tn), lambda i, j, kk: (kk, j))],
        out_specs=pl.BlockSpec((tm, tn), lambda i, j, kk: (i, j)),
        out_shape=jax.ShapeDtypeStruct((m, n), out_dtype),
        scratch_shapes=[pltpu.VMEM((tm, tn), F32)],
        compiler_params=pltpu.CompilerParams(
            dimension_semantics=("parallel", "parallel", "arbitrary"),
            vmem_limit_bytes=_vmem_limit(blocks, _nbytes((tm, tn), F32))),
        name=name,
    )(x, w)


def _gated_proj_kernel(oa_ref, ob_ref, wa_ref, wb_ref, ga_ref, gb_ref, y_ref, wabf_ref, wbbf_ref):
    _cast_weight_once(wa_ref, wabf_ref)
    _cast_weight_once(wb_ref, wbbf_ref)
    pa = jnp.dot(oa_ref[...], wabf_ref[...], preferred_element_type=F32)
    pb = jnp.dot(ob_ref[...], wbbf_ref[...], preferred_element_type=F32)
    y = jax.nn.sigmoid(ga_ref[...]) * pa + jax.nn.sigmoid(gb_ref[...]) * pb
    y_ref[...] = y.astype(y_ref.dtype)


def _gated_proj(o_a, o_b, w_a, w_b, layer, gates, *, tm, tn):
    t, ka = o_a.shape
    _, kb = o_b.shape
    d = w_a.shape[2]
    assert t % tm == 0 and d % tn == 0
    n_n = d // tn
    blocks = (_nbytes((tm, ka), BF16) + _nbytes((tm, kb), BF16) + _nbytes((ka, tn), F32)
              + _nbytes((kb, tn), F32) + 2 * _nbytes((tm, tn), F32) + _nbytes((tm, tn), BF16))
    scratch = _nbytes((ka, tn), BF16) + _nbytes((kb, tn), BF16)
    return pl.pallas_call(
        _gated_proj_kernel,
        grid=(n_n, t // tm),
        in_specs=[pl.BlockSpec((tm, ka), lambda j, i: (i, 0)),
                  pl.BlockSpec((tm, kb), lambda j, i: (i, 0)),
                  pl.BlockSpec((None, ka, tn), lambda j, i: (layer, 0, j)),
                  pl.BlockSpec((None, kb, tn), lambda j, i: (layer, 0, j)),
                  pl.BlockSpec((tm, tn), lambda j, i: (i, j)),
                  pl.BlockSpec((tm, tn), lambda j, i: (i, n_n + j))],
        out_specs=pl.BlockSpec((tm, tn), lambda j, i: (i, j)),
        out_shape=jax.ShapeDtypeStruct((t, d), BF16),
        scratch_shapes=[pltpu.VMEM((ka, tn), BF16), pltpu.VMEM((kb, tn), BF16)],
        compiler_params=pltpu.CompilerParams(
            dimension_semantics=("parallel", "arbitrary"),
            vmem_limit_bytes=_vmem_limit(blocks, scratch)),
        name="gated_proj",
    )(o_a, o_b, w_a, w_b, gates, gates)


def _tile_indices():
    kk = lax.broadcasted_iota(jnp.int32, (ATTN_BLOCK, ATTN_BLOCK), 0)
    qq = lax.broadcasted_iota(jnp.int32, (ATTN_BLOCK, ATTN_BLOCK), 1)
    return kk, qq


def _key_block(k_ref, j, g):
    rows = pl.ds(pl.multiple_of(j * ATTN_BLOCK, ATTN_BLOCK), ATTN_BLOCK)
    return k_ref[rows, g * HEAD_DIM:(g + 1) * HEAD_DIM]


def _moba_kernel(slopes_ref, q_ref, k_ref, vt_ref, o_ref, kmean_ref, alibi_ref, bias_ref, acc_ref,
                 *, scale2, group):
    n_blocks = kmean_ref.shape[0]
    hg = pl.program_id(1)
    i = pl.program_id(2)
    kk, qq = _tile_indices()
    slopes2 = [slopes_ref[hg * group + g] * LOG2E for g in range(group)]

    @pl.when(i == 0)
    def _():
        for j in range(n_blocks):
            kj = k_ref[j * ATTN_BLOCK:(j + 1) * ATTN_BLOCK, :].astype(F32)
            kmean_ref[j:j + 1, :] = jnp.sum(kj, axis=0, keepdims=True) * (1.0 / ATTN_BLOCK)
        dist = (kk - qq).astype(F32)
        for g in range(group):
            alibi_ref[g] = slopes2[g] * dist

    qs = [q_ref[:, g * HEAD_DIM:(g + 1) * HEAD_DIM] for g in range(group)]

    for g in range(group):
        gate = lax.dot_general(kmean_ref[:, g * HEAD_DIM:(g + 1) * HEAD_DIM], qs[g].astype(F32),
                               _NT_DIMS, precision=lax.Precision.HIGHEST,
                               preferred_element_type=F32)
        row = lax.broadcasted_iota(jnp.int32, gate.shape, 0)
        rank = jnp.zeros(gate.shape, jnp.int32)
        for jp in range(n_blocks):
            g_jp = gate[jp:jp + 1, :]
            beats = (g_jp > gate) | ((g_jp == gate) & (jp < row))
            rank = rank + jnp.where(beats, 1, 0) * jnp.where(jp < i, 1, 0)
        selected = (row < i) & (rank < MOBA_TOPK)
        block_off = slopes2[g] * ((row - i) * ATTN_BLOCK).astype(F32)
        bias_ref[g] = jnp.where(selected, block_off, NEG)

    def scores(j):
        return [lax.dot_general(_key_block(k_ref, j, g), qs[g], _NT_DIMS,
                                preferred_element_type=F32) for g in range(group)]

    def values(j, g):
        return vt_ref[j, g * HEAD_DIM:(g + 1) * HEAD_DIM, :]

    causal = kk <= qq
    ss = scores(i)
    stats, ps = [], []
    for g in range(group):
        t = jnp.where(causal, ss[g] * scale2 + alibi_ref[g], NEG)
        m0 = jnp.max(t, axis=0, keepdims=True)
        p = jnp.exp2(t - m0)
        stats += [m0, jnp.sum(p, axis=0, keepdims=True)]
        ps.append(p.astype(BF16))
    for g in range(group):
        acc_ref[g] = jnp.dot(values(i, g), ps[g], preferred_element_type=F32)

    def body(j, carry):
        ss = scores(j)
        out, ps, alphas = [], [], []
        for g in range(group):
            m, l = carry[2 * g], carry[2 * g + 1]
            t = ss[g] * scale2 + alibi_ref[g]
            brow = bias_ref[g, pl.ds(j, 1), :]
            m_new = jnp.maximum(m, jnp.max(t, axis=0, keepdims=True) + brow)
            p = jnp.exp2(t - (m_new - brow))
            alpha = jnp.exp2(m - m_new)
            out += [m_new, alpha * l + jnp.sum(p, axis=0, keepdims=True)]
            ps.append(p.astype(BF16))
            alphas.append(alpha)
        for g in range(group):
            acc_ref[g] = alphas[g] * acc_ref[g] + jnp.dot(values(j, g), ps[g],
                                                          preferred_element_type=F32)
        return tuple(out)

    stats = lax.fori_loop(0, i, body, tuple(stats))
    for g in range(group):
        o = acc_ref[g] * (1.0 / stats[2 * g + 1])
        o_ref[:, g * HEAD_DIM:(g + 1) * HEAD_DIM] = o.T.astype(o_ref.dtype)


def _moba_attention(qk, vt, slopes, *, batch, seq, heads, q_col, k_col, v_row):
    nb = seq // ATTN_BLOCK
    t = batch * seq
    group = math.gcd(heads, MOBA_HEAD_GROUP)
    gw = group * HEAD_DIM
    assert q_col % group == 0 and k_col % group == 0 and v_row % group == 0
    blocks = (2 * _nbytes((ATTN_BLOCK, gw), BF16) + _nbytes((seq, gw), BF16)
              + _nbytes((nb, gw, ATTN_BLOCK), BF16))
    scratch = [pltpu.VMEM((nb, gw), F32),
               pltpu.VMEM((group, ATTN_BLOCK, ATTN_BLOCK), F32),
               pltpu.VMEM((group, nb, ATTN_BLOCK), F32),
               pltpu.VMEM((group, HEAD_DIM, ATTN_BLOCK), F32)]
    scratch_bytes = (_nbytes((nb, gw), F32) + _nbytes((group, ATTN_BLOCK, ATTN_BLOCK), F32)
                     + _nbytes((group, nb, ATTN_BLOCK), F32)
                     + _nbytes((group, HEAD_DIM, ATTN_BLOCK), F32))
    return pl.pallas_call(
        functools.partial(_moba_kernel, scale2=HEAD_DIM ** -0.5 * LOG2E, group=group),
        grid=(batch, heads // group, nb),
        in_specs=[pl.BlockSpec(memory_space=pltpu.SMEM),
                  pl.BlockSpec((ATTN_BLOCK, gw), lambda b, h, i: (b * nb + i, q_col // group + h)),
                  pl.BlockSpec((seq, gw), lambda b, h, i: (b, k_col // group + h)),
                  pl.BlockSpec((nb, gw, ATTN_BLOCK), lambda b, h, i: (b, v_row // group + h, 0))],
        out_specs=pl.BlockSpec((ATTN_BLOCK, gw), lambda b, h, i: (b * nb + i, h)),
        out_shape=jax.ShapeDtypeStruct((t, heads * HEAD_DIM), BF16),
        scratch_shapes=scratch,
        compiler_params=pltpu.CompilerParams(
            dimension_semantics=("parallel", "parallel", "arbitrary"),
            vmem_limit_bytes=_vmem_limit(blocks, scratch_bytes)),
        name="moba_attention",
    )(slopes, qk, qk, vt)


def _diff_kernel(slopes_ref, lamv_ref, g_ref, q_ref, k_ref, vt_ref, o_ref, alibi_ref, acc_ref,
                 *, scale2, lam_init, group):
    hg = pl.program_id(1)
    i = pl.program_id(2)
    kk, qq = _tile_indices()
    slopes2 = [slopes_ref[hg * group + g] * LOG2E for g in range(group)]
    chains = [(g, mi) for g in range(group) for mi in range(2)]

    @pl.when(i == 0)
    def _():
        dist = (kk - qq).astype(F32)
        for g in range(group):
            alibi_ref[g] = slopes2[g] * dist

    qs = [q_ref[:, c * HEAD_DIM:(c + 1) * HEAD_DIM] for c in range(2 * group)]

    def value_block(j, g):
        return vt_ref[j, g * DIFF_V_DIM:(g + 1) * DIFF_V_DIM, :]

    def scores(j):
        return [lax.dot_general(_key_block(k_ref, j, 2 * g + mi), qs[2 * g + mi], _NT_DIMS,
                                preferred_element_type=F32) for g, mi in chains]

    causal = kk <= qq
    ss = scores(i)
    stats, ps = [], []
    for n, (g, mi) in enumerate(chains):
        t = jnp.where(causal, ss[n] * scale2 + alibi_ref[g], NEG)
        m0 = jnp.max(t, axis=0, keepdims=True)
        p = jnp.exp2(t - m0)
        stats += [m0, jnp.sum(p, axis=0, keepdims=True)]
        ps.append(p.astype(BF16))
    for n, (g, mi) in enumerate(chains):
        acc_ref[g, mi] = jnp.dot(value_block(i, g), ps[n], preferred_element_type=F32)

    def body(j, carry):
        ss = scores(j)
        out, ps, alphas = [], [], []
        block_rel = ((j - i) * ATTN_BLOCK).astype(F32)
        for n, (g, mi) in enumerate(chains):
            m, l = carry[2 * n], carry[2 * n + 1]
            t = ss[n] * scale2 + alibi_ref[g]
            off = slopes2[g] * block_rel
            m_new = jnp.maximum(m, jnp.max(t, axis=0, keepdims=True) + off)
            p = jnp.exp2(t - (m_new - off))
            alpha = jnp.exp2(m - m_new)
            out += [m_new, alpha * l + jnp.sum(p, axis=0, keepdims=True)]
            ps.append(p.astype(BF16))
            alphas.append(alpha)
        for n, (g, mi) in enumerate(chains):
            acc_ref[g, mi] = alphas[n] * acc_ref[g, mi] + jnp.dot(
                value_block(j, g), ps[n], preferred_element_type=F32)
        return tuple(out)

    stats = lax.fori_loop(0, i, body, tuple(stats))

    lv = lamv_ref[...]
    lam = (jnp.exp(jnp.sum(lv[0:1] * lv[1:2], axis=-1, keepdims=True))
           - jnp.exp(jnp.sum(lv[2:3] * lv[3:4], axis=-1, keepdims=True)) + lam_init)
    for g in range(group):
        l1, l2 = stats[4 * g + 1], stats[4 * g + 3]
        o = acc_ref[g, 0] * (1.0 / l1) - lam * (acc_ref[g, 1] * (1.0 / l2))
        o = o * lax.rsqrt(jnp.mean(o * o, axis=0, keepdims=True) + EPS)
        o_ref[:, g * DIFF_V_DIM:(g + 1) * DIFF_V_DIM] = (
            o.T * g_ref[...] * (1.0 - lam_init)).astype(o_ref.dtype)


def _diff_attention(qk, vt, slopes, lam_vecs, g_subln, *, batch, seq, heads, q_col, k_col, v_row,
                    lam_init):
    nb = seq // ATTN_BLOCK
    t = batch * seq
    group = math.gcd(heads, DIFF_HEAD_GROUP)
    gw = group * DIFF_V_DIM
    assert q_col % group == 0 and k_col % group == 0 and v_row % group == 0
    blocks = (_nbytes((ATTN_BLOCK, gw), BF16) + _nbytes((seq, gw), BF16)
              + _nbytes((nb, gw, ATTN_BLOCK), BF16) + _nbytes((ATTN_BLOCK, gw), BF16))
    scratch_bytes = (_nbytes((group, ATTN_BLOCK, ATTN_BLOCK), F32)
                     + _nbytes((group, 2, DIFF_V_DIM, ATTN_BLOCK), F32))
    return pl.pallas_call(
        functools.partial(_diff_kernel, scale2=HEAD_DIM ** -0.5 * LOG2E, lam_init=lam_init,
                          group=group),
        grid=(batch, heads // group, nb),
        in_specs=[pl.BlockSpec(memory_space=pltpu.SMEM),
                  pl.BlockSpec((4, HEAD_DIM), lambda b, h, i: (0, 0)),
                  pl.BlockSpec((1, DIFF_V_DIM), lambda b, h, i: (0, 0)),
                  pl.BlockSpec((ATTN_BLOCK, gw), lambda b, h, i: (b * nb + i, q_col // group + h)),
                  pl.BlockSpec((seq, gw), lambda b, h, i: (b, k_col // group + h)),
                  pl.BlockSpec((nb, gw, ATTN_BLOCK), lambda b, h, i: (b, v_row // group + h, 0))],
        out_specs=pl.BlockSpec((ATTN_BLOCK, gw), lambda b, h, i: (b * nb + i, h)),
        out_shape=jax.ShapeDtypeStruct((t, heads * DIFF_V_DIM), BF16),
        scratch_shapes=[pltpu.VMEM((group, ATTN_BLOCK, ATTN_BLOCK), F32),
                        pltpu.VMEM((group, 2, DIFF_V_DIM, ATTN_BLOCK), F32)],
        compiler_params=pltpu.CompilerParams(
            dimension_semantics=("parallel", "parallel", "arbitrary"),
            vmem_limit_bytes=_vmem_limit(blocks, scratch_bytes)),
        name="diff_attention",
    )(slopes, lam_vecs, g_subln.reshape(1, DIFF_V_DIM), qk, qk, vt)


def _alibi_slopes(n_heads):
    return 2.0 ** (-8.0 * jnp.arange(1, n_heads + 1, dtype=F32) / n_heads)


def _tile(n, want):
    if n <= want:
        return n
    t = want
    while n % t:
        t -= ATTN_BLOCK
    assert t > 0
    return t


def kernel(x, w_in, w_proj_a, w_proj_b, w_out, w_up, w_down, g_pre_mix, g_post_mix, g_pre_mlp,
           g_post_mlp, g_subln, lam_q1, lam_k1, lam_q2, lam_k2):
    batch, seq, d = x.shape
    depth = w_in.shape[0]
    t = batch * seq
    mw = d // 2
    heads_a = mw // HEAD_DIM
    heads_b = mw // DIFF_V_DIM
    assert seq % ATTN_BLOCK == 0 and mw % DIFF_V_DIM == 0
    assert w_in.shape[2] == 6 * mw + 2 * d

    slopes_a = _alibi_slopes(heads_a)
    slopes_b = _alibi_slopes(heads_b)
    tm = _tile(t, 1024)
    tn = _tile(mw, 512)
    wt = mw // tn
    qk_tile = lambda j: j + jnp.where(j >= 2 * wt, wt, 0)
    v_tile = lambda j: 2 * wt + j + jnp.where(j >= wt, 2 * wt, 0)
    gate_tile = lambda j: 6 * wt + j
    plain_tile = lambda j: j

    xf = x.reshape(t, d)
    h = _rmsnorm_cast(xf, g_pre_mix[0])
    for l in range(depth):
        qk = _proj(h, w_in, l, qk_tile, 4 * wt, out_dtype=BF16, tm=tm, tn=tn, name="qk_proj")
        vt = _proj(h, w_in, l, v_tile, 2 * wt, out_dtype=BF16, tm=tm, tn=tn, key_blocked_t=True,
                   name="value_t_proj")
        gates = _proj(h, w_in, l, gate_tile, 2 * d // tn, out_dtype=F32, tm=tm, tn=tn,
                      name="gate_proj")

        o_a = _moba_attention(qk, vt, slopes_a, batch=batch, seq=seq, heads=heads_a,
                              q_col=0, k_col=heads_a, v_row=0)
        lam_init = 0.8 - 0.6 * math.exp(-0.3 * l)
        lam_vecs = jnp.stack([lam_q1[l], lam_k1[l], lam_q2[l], lam_k2[l]]).astype(F32)
        o_b = _diff_attention(qk, vt, slopes_b, lam_vecs, g_subln[l], batch=batch, seq=seq,
                              heads=heads_b, q_col=2 * heads_b, k_col=3 * heads_b, v_row=heads_b,
                              lam_init=lam_init)

        y = _gated_proj(o_a, o_b, w_proj_a, w_proj_b, l, gates, tm=tm, tn=tn)
        mix = _proj(y, w_out, l, plain_tile, d // tn, out_dtype=F32, tm=tm, tn=tn, name="out_proj")
        xf, h = _residual(xf, mix, g_post_mix[l], g_pre_mlp[l])

        u = _proj(h, w_up, l, plain_tile, 4 * d // tn, out_dtype=BF16, tm=tm, tn=tn,
                  epilogue=_relu_squared, name="mlp_up")
        m = _matmul_ksplit(u, w_down[l].astype(BF16), out_dtype=F32, tm=tm, tn=_tile(d, 1024),
                           tk=_tile(4 * d, 4096), name="mlp_down")
        g_next = g_pre_mix[l + 1] if l + 1 < depth else None
        xf, h = _residual(xf, m, g_post_mlp[l], g_next)

    return xf.reshape(batch, seq, d)
```

```python
import functools
import math

import jax
import jax.numpy as jnp
from jax import lax
from jax.experimental import pallas as pl
from jax.experimental.pallas import tpu as pltpu

F32 = jnp.float32
BF16 = jnp.bfloat16

V7X_VMEM_BYTES = 64 * 2**20
V7X_VMEM_COMPILER_RESERVE = 12 * 2**20

HEAD_DIM = 128
DIFF_V_DIM = 256
ATTN_BLOCK = 256
MOBA_TOPK = 3
EPS = 1e-6
NEG = -1e30
LOG2E = math.log2(math.e)

MOBA_HEAD_GROUP = 8
DIFF_HEAD_GROUP = 4

_NT_DIMS = (((1,), (1,)), ((), ()))


def _vmem_limit(block_bytes, scratch_bytes=0):
    want = 2 * block_bytes + scratch_bytes + V7X_VMEM_COMPILER_RESERVE
    return int(min(want, V7X_VMEM_BYTES - 2 * 2**20))


def _nbytes(shape, dtype):
    return math.prod(shape) * jnp.dtype(dtype).itemsize


def _rms(x):
    return x * lax.rsqrt(jnp.mean(x * x, axis=-1, keepdims=True) + EPS)


def _rmsnorm_cast_kernel(x_ref, g_ref, h_ref):
    h_ref[...] = (_rms(x_ref[...]) * g_ref[...]).astype(h_ref.dtype)


def _rmsnorm_cast(x, g, *, rows=256):
    t, d = x.shape
    assert t % rows == 0
    blocks = _nbytes((rows, d), F32) + _nbytes((rows, d), BF16)
    return pl.pallas_call(
        _rmsnorm_cast_kernel,
        grid=(t // rows,),
        in_specs=[pl.BlockSpec((rows, d), lambda i: (i, 0)),
                  pl.BlockSpec((1, d), lambda i: (0, 0))],
        out_specs=pl.BlockSpec((rows, d), lambda i: (i, 0)),
        out_shape=jax.ShapeDtypeStruct((t, d), BF16),
        compiler_params=pltpu.CompilerParams(
            dimension_semantics=("parallel",), vmem_limit_bytes=_vmem_limit(blocks)),
        name="rmsnorm_cast",
    )(x, g.reshape(1, d))


def _residual_kernel(x_ref, m_ref, gpost_ref, gnext_ref, xo_ref, h_ref):
    xn = x_ref[...] + _rms(m_ref[...]) * gpost_ref[...]
    xo_ref[...] = xn
    h_ref[...] = (_rms(xn) * gnext_ref[...]).astype(h_ref.dtype)


def _residual_last_kernel(x_ref, m_ref, gpost_ref, xo_ref):
    xo_ref[...] = x_ref[...] + _rms(m_ref[...]) * gpost_ref[...]


def _residual(x, m, g_post, g_next, *, rows=256):
    t, d = x.shape
    assert t % rows == 0
    row_spec = pl.BlockSpec((rows, d), lambda i: (i, 0))
    gain_spec = pl.BlockSpec((1, d), lambda i: (0, 0))
    blocks = 3 * _nbytes((rows, d), F32) + _nbytes((rows, d), BF16)
    params = pltpu.CompilerParams(
        dimension_semantics=("parallel",), vmem_limit_bytes=_vmem_limit(blocks))
    if g_next is None:
        return pl.pallas_call(
            _residual_last_kernel,
            grid=(t // rows,),
            in_specs=[row_spec, row_spec, gain_spec],
            out_specs=row_spec,
            out_shape=jax.ShapeDtypeStruct((t, d), F32),
            compiler_params=params,
            name="residual_last",
        )(x, m, g_post.reshape(1, d)), None
    return pl.pallas_call(
        _residual_kernel,
        grid=(t // rows,),
        in_specs=[row_spec, row_spec, gain_spec, gain_spec],
        out_specs=[row_spec, row_spec],
        out_shape=[jax.ShapeDtypeStruct((t, d), F32), jax.ShapeDtypeStruct((t, d), BF16)],
        compiler_params=params,
        name="residual_norm",
    )(x, m, g_post.reshape(1, d), g_next.reshape(1, d))


def _identity(x):
    return x


def _relu_squared(x):
    r = jnp.maximum(x, 0.0)
    return r * r


def _cast_weight_once(w_ref, wbf_ref, transpose=False):
    @pl.when(pl.program_id(1) == 0)
    def _():
        w = w_ref[...]
        wbf_ref[...] = (w.T if transpose else w).astype(BF16)


def _proj_kernel(x_ref, w_hbm, o_ref, wstage_ref, wbf_ref, sem, *, layer, col_tile, n_tiles,
                 epilogue, key_blocked_t):
    j = pl.program_id(0)
    k, tn = wstage_ref.shape

    def weight_copy(tile):
        col = pl.multiple_of(col_tile(tile) * tn, tn)
        return pltpu.make_async_copy(w_hbm.at[layer, :, pl.ds(col, tn)], wstage_ref, sem)

    @pl.when(pl.program_id(1) == 0)
    def _():
        @pl.when(j == 0)
        def _():
            weight_copy(j).start()

        weight_copy(j).wait()
        if key_blocked_t:
            for c in range(k // tn):
                rows = slice(c * tn, (c + 1) * tn)
                wbf_ref[:, rows] = wstage_ref[rows, :].T.astype(BF16)
        else:
            wbf_ref[...] = wstage_ref[...].astype(BF16)

        @pl.when(j + 1 < n_tiles)
        def _():
            weight_copy(j + 1).start()

    if key_blocked_t:
        rt = epilogue(lax.dot_general(wbf_ref[...], x_ref[...], _NT_DIMS,
                                      preferred_element_type=F32))
        for c in range(o_ref.shape[0]):
            o_ref[c] = rt[:, c * ATTN_BLOCK:(c + 1) * ATTN_BLOCK].astype(o_ref.dtype)
    else:
        r = epilogue(jnp.dot(x_ref[...], wbf_ref[...], preferred_element_type=F32))
        o_ref[...] = r.astype(o_ref.dtype)


def _proj(x, w, layer, col_tile, n_tiles, *, out_dtype, tm, tn, epilogue=_identity,
          key_blocked_t=False, name):
    m, k = x.shape
    assert m % tm == 0 and w.shape[1] == k and w.shape[2] % tn == 0 and k % tn == 0
    if key_blocked_t:
        assert tm % ATTN_BLOCK == 0
        out_shape = jax.ShapeDtypeStruct((m // ATTN_BLOCK, n_tiles * tn, ATTN_BLOCK), out_dtype)
        out_spec = pl.BlockSpec((tm // ATTN_BLOCK, tn, ATTN_BLOCK), lambda j, i: (i, j, 0))
    else:
        out_shape = jax.ShapeDtypeStruct((m, n_tiles * tn), out_dtype)
        out_spec = pl.BlockSpec((tm, tn), lambda j, i: (i, j))
    blocks = _nbytes((tm, k), BF16) + _nbytes((tm, tn), out_dtype)
    scratch_bytes = _nbytes((k, tn), F32) + _nbytes((k, tn), BF16)
    return pl.pallas_call(
        functools.partial(_proj_kernel, layer=layer, col_tile=col_tile, n_tiles=n_tiles,
                          epilogue=epilogue, key_blocked_t=key_blocked_t),
        grid=(n_tiles, m // tm),
        in_specs=[pl.BlockSpec((tm, k), lambda j, i: (i, 0)),
                  pl.BlockSpec(memory_space=pl.ANY)],
        out_specs=out_spec,
        out_shape=out_shape,
        scratch_shapes=[pltpu.VMEM((k, tn), F32),
                        pltpu.VMEM((tn, k) if key_blocked_t else (k, tn), BF16),
                        pltpu.SemaphoreType.DMA(())],
        compiler_params=pltpu.CompilerParams(
            dimension_semantics=("arbitrary", "arbitrary"),
            vmem_limit_bytes=_vmem_limit(blocks, scratch_bytes)),
        name=name,
    )(x, w)


def _cast_kernel(w_ref, o_ref):
    o_ref[...] = w_ref[...].astype(o_ref.dtype)


def _cast_layer_bf16(w, layer, *, rows=512):
    _, k, n = w.shape
    assert k % rows == 0
    blocks = _nbytes((rows, n), F32) + _nbytes((rows, n), BF16)
    return pl.pallas_call(
        _cast_kernel,
        grid=(k // rows,),
        in_specs=[pl.BlockSpec((None, rows, n), lambda i: (layer, i, 0))],
        out_specs=pl.BlockSpec((rows, n), lambda i: (i, 0)),
        out_shape=jax.ShapeDtypeStruct((k, n), BF16),
        compiler_params=pltpu.CompilerParams(
            dimension_semantics=("parallel",), vmem_limit_bytes=_vmem_limit(blocks)),
        name="weight_cast",
    )(w)


def _matmul_ksplit_kernel(x_ref, w_ref, o_ref, acc_ref, *, n_k):
    k = pl.program_id(2)

    @pl.when(k == 0)
    def _():
        acc_ref[...] = jnp.zeros_like(acc_ref)

    acc_ref[...] += jnp.dot(x_ref[...], w_ref[...], preferred_element_type=F32)

    @pl.when(k == n_k - 1)
    def _():
        o_ref[...] = acc_ref[...].astype(o_ref.dtype)


def _matmul_ksplit(x, w, *, out_dtype, tm, tn, tk, name):
    m, k = x.shape
    _, n = w.shape
    assert m % tm == 0 and n % tn == 0 and k % tk == 0
    n_k = k // tk
    blocks = _nbytes((tm, tk), BF16) + _nbytes((tk, tn), BF16) + _nbytes((tm, tn), out_dtype)
    return pl.pallas_call(
        functools.partial(_matmul_ksplit_kernel, n_k=n_k),
        grid=(m // tm, n // tn, n_k),
        in_specs=[pl.BlockSpec((tm, tk), lambda i, j, kk: (i, kk)),
                  pl.BlockSpec((tk, tn), lambda i, j, kk: (kk, j))],
        out_specs=pl.BlockSpec((tm, tn), lambda i, j, kk: (i, j)),
        out_shape=jax.ShapeDtypeStruct((m, n), out_dtype),
        scratch_shapes=[pltpu.VMEM((tm, tn), F32)],
        compiler_params=pltpu.CompilerParams(
            dimension_semantics=("parallel", "parallel", "arbitrary"),
            vmem_limit_bytes=_vmem_limit(blocks, _nbytes((tm, tn), F32))),
        name=name,
    )(x, w)


def _gated_proj_kernel(oa_ref, ob_ref, wa_ref, wb_ref, ga_ref, gb_ref, y_ref, wabf_ref, wbbf_ref):
    _cast_weight_once(wa_ref, wabf_ref)
    _cast_weight_once(wb_ref, wbbf_ref)
    pa = jnp.dot(oa_ref[...], wabf_ref[...], preferred_element_type=F32)
    pb = jnp.dot(ob_ref[...], wbbf_ref[...], preferred_element_type=F32)
    y = jax.nn.sigmoid(ga_ref[...]) * pa + jax.nn.sigmoid(gb_ref[...]) * pb
    y_ref[...] = y.astype(y_ref.dtype)


def _gated_proj(o_a, o_b, w_a, w_b, layer, gates, *, tm, tn):
    t, ka = o_a.shape
    _, kb = o_b.shape
    d = w_a.shape[2]
    assert t % tm == 0 and d % tn == 0
    n_n = d // tn
    blocks = (_nbytes((tm, ka), BF16) + _nbytes((tm, kb), BF16) + _nbytes((ka, tn), F32)
              + _nbytes((kb, tn), F32) + 2 * _nbytes((tm, tn), F32) + _nbytes((tm, tn), BF16))
    scratch = _nbytes((ka, tn), BF16) + _nbytes((kb, tn), BF16)
    return pl.pallas_call(
        _gated_proj_kernel,
        grid=(n_n, t // tm),
        in_specs=[pl.BlockSpec((tm, ka), lambda j, i: (i, 0)),
                  pl.BlockSpec((tm, kb), lambda j, i: (i, 0)),
                  pl.BlockSpec((None, ka, tn), lambda j, i: (layer, 0, j)),
                  pl.BlockSpec((None, kb, tn), lambda j, i: (layer, 0, j)),
                  pl.BlockSpec((tm, tn), lambda j, i: (i, j)),
                  pl.BlockSpec((tm, tn), lambda j, i: (i, n_n + j))],
        out_specs=pl.BlockSpec((tm, tn), lambda j, i: (i, j)),
        out_shape=jax.ShapeDtypeStruct((t, d), BF16),
        scratch_shapes=[pltpu.VMEM((ka, tn), BF16), pltpu.VMEM((kb, tn), BF16)],
        compiler_params=pltpu.CompilerParams(
            dimension_semantics=("parallel", "arbitrary"),
            vmem_limit_bytes=_vmem_limit(blocks, scratch)),
        name="gated_proj",
    )(o_a, o_b, w_a, w_b, gates, gates)


def _tile_indices():
    kk = lax.broadcasted_iota(jnp.int32, (ATTN_BLOCK, ATTN_BLOCK), 0)
    qq = lax.broadcasted_iota(jnp.int32, (ATTN_BLOCK, ATTN_BLOCK), 1)
    return kk, qq


def _key_block(k_ref, j, g):
    rows = pl.ds(pl.multiple_of(j * ATTN_BLOCK, ATTN_BLOCK), ATTN_BLOCK)
    return k_ref[rows, g * HEAD_DIM:(g + 1) * HEAD_DIM]


def _moba_kernel(slopes_ref, q_ref, k_ref, vt_ref, o_ref, kmean_ref, alibi_ref, bias_ref, acc_ref,
                 *, scale2, group):
    n_blocks = kmean_ref.shape[0]
    hg = pl.program_id(1)
    i = pl.program_id(2)
    kk, qq = _tile_indices()
    slopes2 = [slopes_ref[hg * group + g] * LOG2E for g in range(group)]

    @pl.when(i == 0)
    def _():
        for j in range(n_blocks):
            kj = k_ref[j * ATTN_BLOCK:(j + 1) * ATTN_BLOCK, :].astype(F32)
            kmean_ref[j:j + 1, :] = jnp.sum(kj, axis=0, keepdims=True) * (1.0 / ATTN_BLOCK)
        dist = (kk - qq).astype(F32)
        for g in range(group):
            alibi_ref[g] = slopes2[g] * dist

    qs = [q_ref[:, g * HEAD_DIM:(g + 1) * HEAD_DIM] for g in range(group)]

    for g in range(group):
        gate = lax.dot_general(kmean_ref[:, g * HEAD_DIM:(g + 1) * HEAD_DIM], qs[g].astype(F32),
                               _NT_DIMS, precision=lax.Precision.HIGHEST,
                               preferred_element_type=F32)
        row = lax.broadcasted_iota(jnp.int32, gate.shape, 0)
        rank = jnp.zeros(gate.shape, jnp.int32)
        for jp in range(n_blocks):
            g_jp = gate[jp:jp + 1, :]
            beats = (g_jp > gate) | ((g_jp == gate) & (jp < row))
            rank = rank + jnp.where(beats, 1, 0) * jnp.where(jp < i, 1, 0)
        selected = (row < i) & (rank < MOBA_TOPK)
        block_off = slopes2[g] * ((row - i) * ATTN_BLOCK).astype(F32)
        bias_ref[g] = jnp.where(selected, block_off, NEG)

    def scores(j):
        return [lax.dot_general(_key_block(k_ref, j, g), qs[g], _NT_DIMS,
                                preferred_element_type=F32) for g in range(group)]

    def values(j, g):
        return vt_ref[j, g * HEAD_DIM:(g + 1) * HEAD_DIM, :]

    causal = kk <= qq
    ss = scores(i)
    stats, ps = [], []
    for g in range(group):
        t = jnp.where(causal, ss[g] * scale2 + alibi_ref[g], NEG)
        m0 = jnp.max(t, axis=0, keepdims=True)
        p = jnp.exp2(t - m0)
        stats += [m0, jnp.sum(p, axis=0, keepdims=True)]
        ps.append(p.astype(BF16))
    for g in range(group):
        acc_ref[g] = jnp.dot(values(i, g), ps[g], preferred_element_type=F32)

    def body(j, carry):
        ss = scores(j)
        out, ps, alphas = [], [], []
        for g in range(group):
            m, l = carry[2 * g], carry[2 * g + 1]
            t = ss[g] * scale2 + alibi_ref[g]
            brow = bias_ref[g, pl.ds(j, 1), :]
            m_new = jnp.maximum(m, jnp.max(t, axis=0, keepdims=True) + brow)
            p = jnp.exp2(t - (m_new - brow))
            alpha = jnp.exp2(m - m_new)
            out += [m_new, alpha * l + jnp.sum(p, axis=0, keepdims=True)]
            ps.append(p.astype(BF16))
            alphas.append(alpha)
        for g in range(group):
            acc_ref[g] = alphas[g] * acc_ref[g] + jnp.dot(values(j, g), ps[g],
                                                          preferred_element_type=F32)
        return tuple(out)

    stats = lax.fori_loop(0, i, body, tuple(stats))
    for g in range(group):
        o = acc_ref[g] * (1.0 / stats[2 * g + 1])
        o_ref[:, g * HEAD_DIM:(g + 1) * HEAD_DIM] = o.T.astype(o_ref.dtype)


def _moba_attention(qk, vt, slopes, *, batch, seq, heads, q_col, k_col, v_row):
    nb = seq // ATTN_BLOCK
    t = batch * seq
    group = math.gcd(heads, MOBA_HEAD_GROUP)
    gw = group * HEAD_DIM
    assert q_col % group == 0 and k_col % group == 0 and v_row % group == 0
    blocks = (2 * _nbytes((ATTN_BLOCK, gw), BF16) + _nbytes((seq, gw), BF16)
              + _nbytes((nb, gw, ATTN_BLOCK), BF16))
    scratch = [pltpu.VMEM((nb, gw), F32),
               pltpu.VMEM((group, ATTN_BLOCK, ATTN_BLOCK), F32),
               pltpu.VMEM((group, nb, ATTN_BLOCK), F32),
               pltpu.VMEM((group, HEAD_DIM, ATTN_BLOCK), F32)]
    scratch_bytes = (_nbytes((nb, gw), F32) + _nbytes((group, ATTN_BLOCK, ATTN_BLOCK), F32)
                     + _nbytes((group, nb, ATTN_BLOCK), F32)
                     + _nbytes((group, HEAD_DIM, ATTN_BLOCK), F32))
    return pl.pallas_call(
        functools.partial(_moba_kernel, scale2=HEAD_DIM ** -0.5 * LOG2E, group=group),
        grid=(batch, heads // group, nb),
        in_specs=[pl.BlockSpec(memory_space=pltpu.SMEM),
                  pl.BlockSpec((ATTN_BLOCK, gw), lambda b, h, i: (b * nb + i, q_col // group + h)),
                  pl.BlockSpec((seq, gw), lambda b, h, i: (b, k_col // group + h)),
                  pl.BlockSpec((nb, gw, ATTN_BLOCK), lambda b, h, i: (b, v_row // group + h, 0))],
        out_specs=pl.BlockSpec((ATTN_BLOCK, gw), lambda b, h, i: (b * nb + i, h)),
        out_shape=jax.ShapeDtypeStruct((t, heads * HEAD_DIM), BF16),
        scratch_shapes=scratch,
        compiler_params=pltpu.CompilerParams(
            dimension_semantics=("parallel", "parallel", "arbitrary"),
            vmem_limit_bytes=_vmem_limit(blocks, scratch_bytes)),
        name="moba_attention",
    )(slopes, qk, qk, vt)


def _diff_kernel(slopes_ref, lamv_ref, g_ref, q_ref, k_ref, vt_ref, o_ref, alibi_ref, acc_ref,
                 *, scale2, lam_init, group):
    hg = pl.program_id(1)
    i = pl.program_id(2)
    kk, qq = _tile_indices()
    slopes2 = [slopes_ref[hg * group + g] * LOG2E for g in range(group)]
    chains = [(g, mi) for g in range(group) for mi in range(2)]

    @pl.when(i == 0)
    def _():
        dist = (kk - qq).astype(F32)
        for g in range(group):
            alibi_ref[g] = slopes2[g] * dist

    qs = [q_ref[:, c * HEAD_DIM:(c + 1) * HEAD_DIM] for c in range(2 * group)]

    def value_block(j, g):
        return vt_ref[j, g * DIFF_V_DIM:(g + 1) * DIFF_V_DIM, :]

    def scores(j):
        return [lax.dot_general(_key_block(k_ref, j, 2 * g + mi), qs[2 * g + mi], _NT_DIMS,
                                preferred_element_type=F32) for g, mi in chains]

    causal = kk <= qq
    ss = scores(i)
    stats, ps = [], []
    for n, (g, mi) in enumerate(chains):
        t = jnp.where(causal, ss[n] * scale2 + alibi_ref[g], NEG)
        m0 = jnp.max(t, axis=0, keepdims=True)
        p = jnp.exp2(t - m0)
        stats += [m0, jnp.sum(p, axis=0, keepdims=True)]
        ps.append(p.astype(BF16))
    for n, (g, mi) in enumerate(chains):
        acc_ref[g, mi] = jnp.dot(value_block(i, g), ps[n], preferred_element_type=F32)

    def body(j, carry):
        ss = scores(j)
        out, ps, alphas = [], [], []
        block_rel = ((j - i) * ATTN_BLOCK).astype(F32)
        for n, (g, mi) in enumerate(chains):
            m, l = carry[2 * n], carry[2 * n + 1]
            t = ss[n] * scale2 + alibi_ref[g]
            off = slopes2[g] * block_rel
            m_new = jnp.maximum(m, jnp.max(t, axis=0, keepdims=True) + off)
            p = jnp.exp2(t - (m_new - off))
            alpha = jnp.exp2(m - m_new)
            out += [m_new, alpha * l + jnp.sum(p, axis=0, keepdims=True)]
            ps.append(p.astype(BF16))
            alphas.append(alpha)
        for n, (g, mi) in enumerate(chains):
            acc_ref[g, mi] = alphas[n] * acc_ref[g, mi] + jnp.dot(
                value_block(j, g), ps[n], preferred_element_type=F32)
        return tuple(out)

    stats = lax.fori_loop(0, i, body, tuple(stats))

    lv = lamv_ref[...]
    lam = (jnp.exp(jnp.sum(lv[0:1] * lv[1:2], axis=-1, keepdims=True))
           - jnp.exp(jnp.sum(lv[2:3] * lv[3:4], axis=-1, keepdims=True)) + lam_init)
    for g in range(group):
        l1, l2 = stats[4 * g + 1], stats[4 * g + 3]
        o = acc_ref[g, 0] * (1.0 / l1) - lam * (acc_ref[g, 1] * (1.0 / l2))
        o = o * lax.rsqrt(jnp.mean(o * o, axis=0, keepdims=True) + EPS)
        o_ref[:, g * DIFF_V_DIM:(g + 1) * DIFF_V_DIM] = (
            o.T * g_ref[...] * (1.0 - lam_init)).astype(o_ref.dtype)


def _diff_attention(qk, vt, slopes, lam_vecs, g_subln, *, batch, seq, heads, q_col, k_col, v_row,
                    lam_init):
    nb = seq // ATTN_BLOCK
    t = batch * seq
    group = math.gcd(heads, DIFF_HEAD_GROUP)
    gw = group * DIFF_V_DIM
    assert q_col % group == 0 and k_col % group == 0 and v_row % group == 0
    blocks = (_nbytes((ATTN_BLOCK, gw), BF16) + _nbytes((seq, gw), BF16)
              + _nbytes((nb, gw, ATTN_BLOCK), BF16) + _nbytes((ATTN_BLOCK, gw), BF16))
    scratch_bytes = (_nbytes((group, ATTN_BLOCK, ATTN_BLOCK), F32)
                     + _nbytes((group, 2, DIFF_V_DIM, ATTN_BLOCK), F32))
    return pl.pallas_call(
        functools.partial(_diff_kernel, scale2=HEAD_DIM ** -0.5 * LOG2E, lam_init=lam_init,
                          group=group),
        grid=(batch, heads // group, nb),
        in_specs=[pl.BlockSpec(memory_space=pltpu.SMEM),
                  pl.BlockSpec((4, HEAD_DIM), lambda b, h, i: (0, 0)),
                  pl.BlockSpec((1, DIFF_V_DIM), lambda b, h, i: (0, 0)),
                  pl.BlockSpec((ATTN_BLOCK, gw), lambda b, h, i: (b * nb + i, q_col // group + h)),
                  pl.BlockSpec((seq, gw), lambda b, h, i: (b, k_col // group + h)),
                  pl.BlockSpec((nb, gw, ATTN_BLOCK), lambda b, h, i: (b, v_row // group + h, 0))],
        out_specs=pl.BlockSpec((ATTN_BLOCK, gw), lambda b, h, i: (b * nb + i, h)),
        out_shape=jax.ShapeDtypeStruct((t, heads * DIFF_V_DIM), BF16),
        scratch_shapes=[pltpu.VMEM((group, ATTN_BLOCK, ATTN_BLOCK), F32),
                        pltpu.VMEM((group, 2, DIFF_V_DIM, ATTN_BLOCK), F32)],
        compiler_params=pltpu.CompilerParams(
            dimension_semantics=("parallel", "parallel", "arbitrary"),
            vmem_limit_bytes=_vmem_limit(blocks, scratch_bytes)),
        name="diff_attention",
    )(slopes, lam_vecs, g_subln.reshape(1, DIFF_V_DIM), qk, qk, vt)


def _alibi_slopes(n_heads):
    return 2.0 ** (-8.0 * jnp.arange(1, n_heads + 1, dtype=F32) / n_heads)


def _tile(n, want):
    if n <= want:
        return n
    t = want
    while n % t:
        t -= ATTN_BLOCK
    assert t > 0
    return t


def kernel(x, w_in, w_proj_a, w_proj_b, w_out, w_up, w_down, g_pre_mix, g_post_mix, g_pre_mlp,
           g_post_mlp, g_subln, lam_q1, lam_k1, lam_q2, lam_k2):
    batch, seq, d = x.shape
    depth = w_in.shape[0]
    t = batch * seq
    mw = d // 2
    heads_a = mw // HEAD_DIM
    heads_b = mw // DIFF_V_DIM
    assert seq % ATTN_BLOCK == 0 and mw % DIFF_V_DIM == 0
    assert w_in.shape[2] == 6 * mw + 2 * d

    slopes_a = _alibi_slopes(heads_a)
    slopes_b = _alibi_slopes(heads_b)
    tm = _tile(t, 1024)
    tn = _tile(mw, 1024)
    tn_gated = _tile(mw, 512)
    wt = mw // tn
    qk_tile = lambda j: j + jnp.where(j >= 2 * wt, wt, 0)
    v_tile = lambda j: 2 * wt + j + jnp.where(j >= wt, 2 * wt, 0)
    gate_tile = lambda j: 6 * wt + j
    plain_tile = lambda j: j

    xf = x.reshape(t, d)
    h = _rmsnorm_cast(xf, g_pre_mix[0])
    for l in range(depth):
        qk = _proj(h, w_in, l, qk_tile, 4 * wt, out_dtype=BF16, tm=tm, tn=tn, name="qk_proj")
        vt = _proj(h, w_in, l, v_tile, 2 * wt, out_dtype=BF16, tm=tm, tn=tn, key_blocked_t=True,
                   name="value_t_proj")
        gates = _proj(h, w_in, l, gate_tile, 2 * d // tn, out_dtype=F32, tm=tm, tn=tn,
                      name="gate_proj")

        o_a = _moba_attention(qk, vt, slopes_a, batch=batch, seq=seq, heads=heads_a,
                              q_col=0, k_col=heads_a, v_row=0)
        lam_init = 0.8 - 0.6 * math.exp(-0.3 * l)
        lam_vecs = jnp.stack([lam_q1[l], lam_k1[l], lam_q2[l], lam_k2[l]]).astype(F32)
        o_b = _diff_attention(qk, vt, slopes_b, lam_vecs, g_subln[l], batch=batch, seq=seq,
                              heads=heads_b, q_col=2 * heads_b, k_col=3 * heads_b, v_row=heads_b,
                              lam_init=lam_init)

        y = _gated_proj(o_a, o_b, w_proj_a, w_proj_b, l, gates, tm=tm, tn=tn_gated)
        mix = _proj(y, w_out, l, plain_tile, d // tn, out_dtype=F32, tm=tm, tn=tn, name="out_proj")
        xf, h = _residual(xf, mix, g_post_mix[l], g_pre_mlp[l])

        u = _proj(h, w_up, l, plain_tile, 4 * d // tn, out_dtype=BF16, tm=tm, tn=tn,
                  epilogue=_relu_squared, name="mlp_up")
        m = _matmul_ksplit(u, _cast_layer_bf16(w_down, l), out_dtype=F32, tm=tm, tn=_tile(d, 1024),
                           tk=_tile(4 * d, 4096), name="mlp_down")
        g_next = g_pre_mix[l + 1] if l + 1 < depth else None
        xf, h = _residual(xf, m, g_post_mlp[l], g_next)

    return xf.reshape(batch, seq, d)
```

```python
import functools
import math

import jax
import jax.numpy as jnp
from jax import lax
from jax.experimental import pallas as pl
from jax.experimental.pallas import tpu as pltpu

F32 = jnp.float32
BF16 = jnp.bfloat16

V7X_VMEM_BYTES = 64 * 2**20
V7X_VMEM_COMPILER_RESERVE = 12 * 2**20

HEAD_DIM = 128
DIFF_V_DIM = 256
ATTN_BLOCK = 256
MOBA_TOPK = 3
EPS = 1e-6
NEG = -1e30
LOG2E = math.log2(math.e)

MOBA_HEAD_GROUP = 8
DIFF_HEAD_GROUP = 4
SUM_ROWS = 16

_NT_DIMS = (((1,), (1,)), ((), ()))


def _vmem_limit(block_bytes, scratch_bytes=0):
    want = 2 * block_bytes + scratch_bytes + V7X_VMEM_COMPILER_RESERVE
    return int(min(want, V7X_VMEM_BYTES - 2 * 2**20))


def _nbytes(shape, dtype):
    return math.prod(shape) * jnp.dtype(dtype).itemsize


def _rms(x):
    return x * lax.rsqrt(jnp.mean(x * x, axis=-1, keepdims=True) + EPS)


def _rmsnorm_cast_kernel(x_ref, g_ref, h_ref):
    h_ref[...] = (_rms(x_ref[...]) * g_ref[...]).astype(h_ref.dtype)


def _rmsnorm_cast(x, g, *, rows=256):
    t, d = x.shape
    assert t % rows == 0
    blocks = _nbytes((rows, d), F32) + _nbytes((rows, d), BF16)
    return pl.pallas_call(
        _rmsnorm_cast_kernel,
        grid=(t // rows,),
        in_specs=[pl.BlockSpec((rows, d), lambda i: (i, 0)),
                  pl.BlockSpec((1, d), lambda i: (0, 0))],
        out_specs=pl.BlockSpec((rows, d), lambda i: (i, 0)),
        out_shape=jax.ShapeDtypeStruct((t, d), BF16),
        compiler_params=pltpu.CompilerParams(
            dimension_semantics=("parallel",), vmem_limit_bytes=_vmem_limit(blocks)),
        name="rmsnorm_cast",
    )(x, g.reshape(1, d))


def _residual_kernel(x_ref, m_ref, gpost_ref, gnext_ref, xo_ref, h_ref):
    xn = x_ref[...] + _rms(m_ref[...]) * gpost_ref[...]
    xo_ref[...] = xn
    h_ref[...] = (_rms(xn) * gnext_ref[...]).astype(h_ref.dtype)


def _residual_last_kernel(x_ref, m_ref, gpost_ref, xo_ref):
    xo_ref[...] = x_ref[...] + _rms(m_ref[...]) * gpost_ref[...]


def _residual(x, m, g_post, g_next, *, rows=256):
    t, d = x.shape
    assert t % rows == 0
    row_spec = pl.BlockSpec((rows, d), lambda i: (i, 0))
    gain_spec = pl.BlockSpec((1, d), lambda i: (0, 0))
    blocks = 3 * _nbytes((rows, d), F32) + _nbytes((rows, d), BF16)
    params = pltpu.CompilerParams(
        dimension_semantics=("parallel",), vmem_limit_bytes=_vmem_limit(blocks))
    if g_next is None:
        return pl.pallas_call(
            _residual_last_kernel,
            grid=(t // rows,),
            in_specs=[row_spec, row_spec, gain_spec],
            out_specs=row_spec,
            out_shape=jax.ShapeDtypeStruct((t, d), F32),
            compiler_params=params,
            name="residual_last",
        )(x, m, g_post.reshape(1, d)), None
    return pl.pallas_call(
        _residual_kernel,
        grid=(t // rows,),
        in_specs=[row_spec, row_spec, gain_spec, gain_spec],
        out_specs=[row_spec, row_spec],
        out_shape=[jax.ShapeDtypeStruct((t, d), F32), jax.ShapeDtypeStruct((t, d), BF16)],
        compiler_params=params,
        name="residual_norm",
    )(x, m, g_post.reshape(1, d), g_next.reshape(1, d))


def _identity(x, tile):
    return x


def _relu_squared(x, tile):
    r = jnp.maximum(x, 0.0)
    return r * r


def _cast_weight_once(w_ref, wbf_ref, transpose=False):
    @pl.when(pl.program_id(1) == 0)
    def _():
        w = w_ref[...]
        wbf_ref[...] = (w.T if transpose else w).astype(BF16)


def _proj_kernel(x_ref, w_hbm, o_ref, wstage_ref, wbf_ref, sem, *, layer, col_tile, n_tiles,
                 epilogue, key_blocked_t):
    j = pl.program_id(0)
    k, tn = wstage_ref.shape

    def weight_copy(tile):
        col = pl.multiple_of(col_tile(tile) * tn, tn)
        return pltpu.make_async_copy(w_hbm.at[layer, :, pl.ds(col, tn)], wstage_ref, sem)

    @pl.when(pl.program_id(1) == 0)
    def _():
        @pl.when(j == 0)
        def _():
            weight_copy(j).start()

        weight_copy(j).wait()
        if key_blocked_t:
            for c in range(k // tn):
                rows = slice(c * tn, (c + 1) * tn)
                wbf_ref[:, rows] = wstage_ref[rows, :].T.astype(BF16)
        else:
            wbf_ref[...] = wstage_ref[...].astype(BF16)

        @pl.when(j + 1 < n_tiles)
        def _():
            weight_copy(j + 1).start()

    if key_blocked_t:
        rt = epilogue(lax.dot_general(wbf_ref[...], x_ref[...], _NT_DIMS,
                                      preferred_element_type=F32), j)
        n_chunks, n_heads, rows, _ = o_ref.shape
        rows -= SUM_ROWS
        ones = jnp.ones((SUM_ROWS, ATTN_BLOCK), o_ref.dtype)
        for c in range(n_chunks):
            for hh in range(n_heads):
                o_ref[c, hh, :rows, :] = rt[hh * rows:(hh + 1) * rows,
                                            c * ATTN_BLOCK:(c + 1) * ATTN_BLOCK].astype(o_ref.dtype)
                o_ref[c, hh, rows:, :] = ones
    else:
        r = epilogue(jnp.dot(x_ref[...], wbf_ref[...], preferred_element_type=F32), j)
        o_ref[...] = r.astype(o_ref.dtype)


def _proj_and_cast_kernel(x_ref, w_hbm, side_ref, o_ref, side_bf_ref, *scratch, **static):
    side_bf_ref[...] = side_ref[...].astype(side_bf_ref.dtype)
    _proj_kernel(x_ref, w_hbm, o_ref, *scratch, **static)


def _proj(x, w, layer, col_tile, n_tiles, *, out_dtype, tm, tn, epilogue=_identity,
          value_rows=None, also_cast=None, name):
    m, k = x.shape
    assert m % tm == 0 and w.shape[1] == k and w.shape[2] % tn == 0 and k % tn == 0
    n_m = m // tm
    key_blocked_t = value_rows is not None
    if key_blocked_t:
        assert tm % ATTN_BLOCK == 0 and tn % value_rows == 0
        hpt = tn // value_rows
        block = (tm // ATTN_BLOCK, hpt, value_rows + SUM_ROWS, ATTN_BLOCK)
        out_shape = jax.ShapeDtypeStruct((m // ATTN_BLOCK, n_tiles * hpt) + block[2:], out_dtype)
        out_spec = pl.BlockSpec(block, lambda j, i: (i, j, 0, 0))
    else:
        block = (tm, tn)
        out_shape = jax.ShapeDtypeStruct((m, n_tiles * tn), out_dtype)
        out_spec = pl.BlockSpec(block, lambda j, i: (i, j))
    blocks = _nbytes((tm, k), BF16) + _nbytes(block, out_dtype)
    scratch_bytes = _nbytes((k, tn), F32) + _nbytes((k, tn), BF16)
    body = _proj_kernel
    operands = [x, w]
    in_specs = [pl.BlockSpec((tm, k), lambda j, i: (i, 0)), pl.BlockSpec(memory_space=pl.ANY)]
    if also_cast is not None:
        _, side_k, side_n = also_cast.shape
        assert side_k % (n_tiles * n_m) == 0
        slab = (side_k // (n_tiles * n_m), side_n)
        body = _proj_and_cast_kernel
        operands.append(also_cast)
        in_specs.append(pl.BlockSpec((None,) + slab, lambda j, i: (layer, j * n_m + i, 0)))
        out_spec = [out_spec, pl.BlockSpec(slab, lambda j, i: (j * n_m + i, 0))]
        out_shape = [out_shape, jax.ShapeDtypeStruct((side_k, side_n), BF16)]
        blocks += _nbytes(slab, F32) + _nbytes(slab, BF16)
    return pl.pallas_call(
        functools.partial(body, layer=layer, col_tile=col_tile, n_tiles=n_tiles,
                          epilogue=epilogue, key_blocked_t=key_blocked_t),
        grid=(n_tiles, n_m),
        in_specs=in_specs,
        out_specs=out_spec,
        out_shape=out_shape,
        scratch_shapes=[pltpu.VMEM((k, tn), F32),
                        pltpu.VMEM((tn, k) if key_blocked_t else (k, tn), BF16),
                        pltpu.SemaphoreType.DMA(())],
        compiler_params=pltpu.CompilerParams(
            dimension_semantics=("arbitrary", "arbitrary"),
            vmem_limit_bytes=_vmem_limit(blocks, scratch_bytes)),
        name=name,
    )(*operands)


def _matmul_ksplit_kernel(x_ref, w_ref, o_ref, acc_ref, *, n_k):
    k = pl.program_id(2)

    @pl.when(k == 0)
    def _():
        acc_ref[...] = jnp.zeros_like(acc_ref)

    acc_ref[...] += jnp.dot(x_ref[...], w_ref[...], preferred_element_type=F32)

    @pl.when(k == n_k - 1)
    def _():
        o_ref[...] = acc_ref[...].astype(o_ref.dtype)


def _matmul_ksplit(x, w, *, out_dtype, tm, tn, tk, name):
    m, k = x.shape
    _, n = w.shape
    assert m % tm == 0 and n % tn == 0 and k % tk == 0
    n_k = k // tk
    blocks = _nbytes((tm, tk), BF16) + _nbytes((tk, tn), BF16) + _nbytes((tm, tn), out_dtype)
    return pl.pallas_call(
        functools.partial(_matmul_ksplit_kernel, n_k=n_k),
        grid=(m // tm, n // tn, n_k),
        in_specs=[pl.BlockSpec((tm, tk), lambda i, j, kk: (i, kk)),
                  pl.BlockSpec((tk, tn), lambda i, j, kk: (kk, j))],
        out_specs=pl.BlockSpec((tm, tn), lambda i, j, kk: (i, j)),
        out_shape=jax.ShapeDtypeStruct((m, n), out_dtype),
        scratch_shapes=[pltpu.VMEM((tm, tn), F32)],
        compiler_params=pltpu.CompilerParams(
            dimension_semantics=("parallel", "parallel", "arbitrary"),
            vmem_limit_bytes=_vmem_limit(blocks, _nbytes((tm, tn), F32))),
        name=name,
    )(x, w)


def _gated_proj_kernel(oa_ref, ob_ref, wa_ref, wb_ref, ga_ref, gb_ref, y_ref, wabf_ref, wbbf_ref):
    _cast_weight_once(wa_ref, wabf_ref)
    _cast_weight_once(wb_ref, wbbf_ref)
    pa = jnp.dot(oa_ref[...], wabf_ref[...], preferred_element_type=F32)
    pb = jnp.dot(ob_ref[...], wbbf_ref[...], preferred_element_type=F32)
    y = jax.nn.sigmoid(ga_ref[...]) * pa + jax.nn.sigmoid(gb_ref[...]) * pb
    y_ref[...] = y.astype(y_ref.dtype)


def _gated_proj(o_a, o_b, w_a, w_b, layer, gates, *, tm, tn):
    t, ka = o_a.shape
    _, kb = o_b.shape
    d = w_a.shape[2]
    assert t % tm == 0 and d % tn == 0
    n_n = d // tn
    blocks = (_nbytes((tm, ka), BF16) + _nbytes((tm, kb), BF16) + _nbytes((ka, tn), F32)
              + _nbytes((kb, tn), F32) + 2 * _nbytes((tm, tn), F32) + _nbytes((tm, tn), BF16))
    scratch = _nbytes((ka, tn), BF16) + _nbytes((kb, tn), BF16)
    return pl.pallas_call(
        _gated_proj_kernel,
        grid=(n_n, t // tm),
        in_specs=[pl.BlockSpec((tm, ka), lambda j, i: (i, 0)),
                  pl.BlockSpec((tm, kb), lambda j, i: (i, 0)),
                  pl.BlockSpec((None, ka, tn), lambda j, i: (layer, 0, j)),
                  pl.BlockSpec((None, kb, tn), lambda j, i: (layer, 0, j)),
                  pl.BlockSpec((tm, tn), lambda j, i: (i, j)),
                  pl.BlockSpec((tm, tn), lambda j, i: (i, n_n + j))],
        out_specs=pl.BlockSpec((tm, tn), lambda j, i: (i, j)),
        out_shape=jax.ShapeDtypeStruct((t, d), BF16),
        scratch_shapes=[pltpu.VMEM((ka, tn), BF16), pltpu.VMEM((kb, tn), BF16)],
        compiler_params=pltpu.CompilerParams(
            dimension_semantics=("parallel", "arbitrary"),
            vmem_limit_bytes=_vmem_limit(blocks, scratch)),
        name="gated_proj",
    )(o_a, o_b, w_a, w_b, gates, gates)


def _tile_indices():
    kk = lax.broadcasted_iota(jnp.int32, (ATTN_BLOCK, ATTN_BLOCK), 0)
    qq = lax.broadcasted_iota(jnp.int32, (ATTN_BLOCK, ATTN_BLOCK), 1)
    return kk, qq


def _key_block(k_ref, j, g):
    rows = pl.ds(pl.multiple_of(j * ATTN_BLOCK, ATTN_BLOCK), ATTN_BLOCK)
    return k_ref[rows, g * HEAD_DIM:(g + 1) * HEAD_DIM]


def _store_alibi_tiles(alibi_ref, slopes2):
    kk, qq = _tile_indices()
    dist = (kk - qq).astype(F32)
    for g, slope2 in enumerate(slopes2):
        alibi_ref[0, g] = slope2 * dist
        alibi_ref[1, g] = jnp.where(kk <= qq, slope2 * dist, NEG)


def _flash_blocks(i, n_units, *, scores, values, alibi, bias, acc_ref):
    units = range(n_units)

    def sweep(rs, ms):
        ms = list(ms)
        blocks = [jnp.where(r == 0, i, r - 1) for r in rs]
        owns = [jnp.where(r == 0, 1, 0) for r in rs]
        ss = [[scores(j, u) for u in units] for j in blocks]
        ps, alphas = [], []
        for n, j in enumerate(blocks):
            ps.append([])
            alphas.append([])
            for u in units:
                t = ss[n][u] + alibi(u, owns[n])
                b = bias(j, u)
                m_new = jnp.maximum(ms[u], jnp.max(t, axis=0, keepdims=True) + b)
                ps[n].append(jnp.exp2(t - (m_new - b)).astype(BF16))
                alphas[n].append(jnp.exp2(ms[u] - m_new))
                ms[u] = m_new
        for u in units:
            acc = acc_ref[u]
            for n, j in enumerate(blocks):
                acc = alphas[n][u] * acc + jnp.dot(values(j, u), ps[n][u],
                                                   preferred_element_type=F32)
            acc_ref[u] = acc
        return tuple(ms)

    for u in units:
        acc_ref[u] = jnp.zeros(acc_ref.shape[1:], acc_ref.dtype)
    ms = tuple(jnp.full((1, ATTN_BLOCK), NEG, F32) for _ in units)

    n_blocks = i + 1
    ms = lax.fori_loop(0, n_blocks // 2, lambda k, c: sweep([2 * k, 2 * k + 1], c), ms)

    @pl.when(lax.rem(n_blocks, 2) == 1)
    def _():
        sweep([i], ms)


def _moba_kernel(slopes_ref, q_ref, k_ref, vt_ref, o_ref, kmean_ref, alibi_ref, bias_ref, acc_ref,
                 *, group):
    n_blocks = kmean_ref.shape[0]
    hg = pl.program_id(1)
    i = pl.program_id(2)
    slopes2 = [slopes_ref[hg * group + g] * LOG2E for g in range(group)]

    @pl.when(i == 0)
    def _():
        for j in range(n_blocks):
            kj = k_ref[j * ATTN_BLOCK:(j + 1) * ATTN_BLOCK, :].astype(F32)
            kmean_ref[j:j + 1, :] = jnp.sum(kj, axis=0, keepdims=True) * (1.0 / ATTN_BLOCK)
        _store_alibi_tiles(alibi_ref, slopes2)

    qs = [q_ref[:, g * HEAD_DIM:(g + 1) * HEAD_DIM] for g in range(group)]

    for g in range(group):
        gate = lax.dot_general(kmean_ref[:, g * HEAD_DIM:(g + 1) * HEAD_DIM], qs[g].astype(F32),
                               _NT_DIMS, precision=lax.Precision.HIGHEST,
                               preferred_element_type=F32)
        row = lax.broadcasted_iota(jnp.int32, gate.shape, 0)
        rank = jnp.zeros(gate.shape, jnp.int32)
        for jp in range(n_blocks):
            g_jp = gate[jp:jp + 1, :]
            beats = (g_jp > gate) | ((g_jp == gate) & (jp < row))
            rank = rank + jnp.where(beats, 1, 0) * jnp.where(jp < i, 1, 0)
        selected = (row < i) & (rank < MOBA_TOPK)
        block_off = slopes2[g] * ((row - i) * ATTN_BLOCK).astype(F32)
        bias_ref[g] = jnp.where(selected | (row == i), block_off, NEG)

    _flash_blocks(
        i, group,
        scores=lambda j, g: lax.dot_general(_key_block(k_ref, j, g), qs[g], _NT_DIMS,
                                            preferred_element_type=F32),
        values=lambda j, g: vt_ref[j, g],
        alibi=lambda g, own: alibi_ref[own, g],
        bias=lambda j, g: bias_ref[g, pl.ds(j, 1), :],
        acc_ref=acc_ref)
    for g in range(group):
        o = acc_ref[g, :HEAD_DIM, :] * (1.0 / acc_ref[g, HEAD_DIM:HEAD_DIM + 1, :])
        o_ref[:, g * HEAD_DIM:(g + 1) * HEAD_DIM] = o.T.astype(o_ref.dtype)


def _moba_attention(qk, vt, slopes, *, batch, seq, heads, q_col, k_col):
    nb = seq // ATTN_BLOCK
    t = batch * seq
    group = math.gcd(heads, MOBA_HEAD_GROUP)
    gw = group * HEAD_DIM
    v_rows = HEAD_DIM + SUM_ROWS
    assert q_col % group == 0 and k_col % group == 0 and vt.shape[1:] == (heads, v_rows, ATTN_BLOCK)
    blocks = (2 * _nbytes((ATTN_BLOCK, gw), BF16) + _nbytes((seq, gw), BF16)
              + _nbytes((nb, group, v_rows, ATTN_BLOCK), BF16))
    scratch_types = [((nb, gw), F32), ((2, group, ATTN_BLOCK, ATTN_BLOCK), F32),
                     ((group, nb, ATTN_BLOCK), F32), ((group, v_rows, ATTN_BLOCK), F32)]
    scratch = [pltpu.VMEM(shape, dtype) for shape, dtype in scratch_types]
    scratch_bytes = sum(_nbytes(shape, dtype) for shape, dtype in scratch_types)
    return pl.pallas_call(
        functools.partial(_moba_kernel, group=group),
        grid=(batch, heads // group, nb),
        in_specs=[pl.BlockSpec(memory_space=pltpu.SMEM),
                  pl.BlockSpec((ATTN_BLOCK, gw), lambda b, h, i: (b * nb + i, q_col // group + h)),
                  pl.BlockSpec((seq, gw), lambda b, h, i: (b, k_col // group + h)),
                  pl.BlockSpec((nb, group, v_rows, ATTN_BLOCK), lambda b, h, i: (b, h, 0, 0))],
        out_specs=pl.BlockSpec((ATTN_BLOCK, gw), lambda b, h, i: (b * nb + i, h)),
        out_shape=jax.ShapeDtypeStruct((t, heads * HEAD_DIM), BF16),
        scratch_shapes=scratch,
        compiler_params=pltpu.CompilerParams(
            dimension_semantics=("parallel", "parallel", "arbitrary"),
            vmem_limit_bytes=_vmem_limit(blocks, scratch_bytes)),
        name="moba_attention",
    )(slopes, qk, qk, vt)


def _diff_kernel(slopes_ref, lamv_ref, g_ref, q_ref, k_ref, vt_ref, o_ref, alibi_ref, acc_ref,
                 *, lam_init, group):
    hg = pl.program_id(1)
    i = pl.program_id(2)
    slopes2 = [slopes_ref[hg * group + g] * LOG2E for g in range(group)]

    @pl.when(i == 0)
    def _():
        _store_alibi_tiles(alibi_ref, slopes2)

    qs = [q_ref[:, c * HEAD_DIM:(c + 1) * HEAD_DIM] for c in range(2 * group)]

    _flash_blocks(
        i, 2 * group,
        scores=lambda j, c: lax.dot_general(_key_block(k_ref, j, c), qs[c], _NT_DIMS,
                                            preferred_element_type=F32),
        values=lambda j, c: vt_ref[j, c // 2],
        alibi=lambda c, own: alibi_ref[own, c // 2],
        bias=lambda j, c: slopes2[c // 2] * ((j - i) * ATTN_BLOCK).astype(F32),
        acc_ref=acc_ref)

    lv = lamv_ref[...]
    lam = (jnp.exp(jnp.sum(lv[0:1] * lv[1:2], axis=-1, keepdims=True))
           - jnp.exp(jnp.sum(lv[2:3] * lv[3:4], axis=-1, keepdims=True)) + lam_init)

    def normalised(c):
        return acc_ref[c, :DIFF_V_DIM, :] * (1.0 / acc_ref[c, DIFF_V_DIM:DIFF_V_DIM + 1, :])

    for g in range(group):
        o = normalised(2 * g) - lam * normalised(2 * g + 1)
        o = o * lax.rsqrt(jnp.mean(o * o, axis=0, keepdims=True) + EPS)
        o_ref[:, g * DIFF_V_DIM:(g + 1) * DIFF_V_DIM] = (
            o.T * g_ref[...] * (1.0 - lam_init)).astype(o_ref.dtype)


def _diff_attention(qk, vt, slopes, lam_vecs, g_subln, *, batch, seq, heads, q_col, k_col,
                    lam_init):
    nb = seq // ATTN_BLOCK
    t = batch * seq
    group = math.gcd(heads, DIFF_HEAD_GROUP)
    gw = group * DIFF_V_DIM
    v_rows = DIFF_V_DIM + SUM_ROWS
    assert q_col % group == 0 and k_col % group == 0 and vt.shape[1:] == (heads, v_rows, ATTN_BLOCK)
    blocks = (_nbytes((ATTN_BLOCK, gw), BF16) + _nbytes((seq, gw), BF16)
              + _nbytes((nb, group, v_rows, ATTN_BLOCK), BF16) + _nbytes((ATTN_BLOCK, gw), BF16))
    scratch_types = [((2, group, ATTN_BLOCK, ATTN_BLOCK), F32),
                     ((2 * group, v_rows, ATTN_BLOCK), F32)]
    scratch_bytes = sum(_nbytes(shape, dtype) for shape, dtype in scratch_types)
    return pl.pallas_call(
        functools.partial(_diff_kernel, lam_init=lam_init, group=group),
        grid=(batch, heads // group, nb),
        in_specs=[pl.BlockSpec(memory_space=pltpu.SMEM),
                  pl.BlockSpec((4, HEAD_DIM), lambda b, h, i: (0, 0)),
                  pl.BlockSpec((1, DIFF_V_DIM), lambda b, h, i: (0, 0)),
                  pl.BlockSpec((ATTN_BLOCK, gw), lambda b, h, i: (b * nb + i, q_col // group + h)),
                  pl.BlockSpec((seq, gw), lambda b, h, i: (b, k_col // group + h)),
                  pl.BlockSpec((nb, group, v_rows, ATTN_BLOCK), lambda b, h, i: (b, h, 0, 0))],
        out_specs=pl.BlockSpec((ATTN_BLOCK, gw), lambda b, h, i: (b * nb + i, h)),
        out_shape=jax.ShapeDtypeStruct((t, heads * DIFF_V_DIM), BF16),
        scratch_shapes=[pltpu.VMEM(shape, dtype) for shape, dtype in scratch_types],
        compiler_params=pltpu.CompilerParams(
            dimension_semantics=("parallel", "parallel", "arbitrary"),
            vmem_limit_bytes=_vmem_limit(blocks, scratch_bytes)),
        name="diff_attention",
    )(slopes, lam_vecs, g_subln.reshape(1, DIFF_V_DIM), qk, qk, vt)


def _alibi_slopes(n_heads):
    return 2.0 ** (-8.0 * jnp.arange(1, n_heads + 1, dtype=F32) / n_heads)


def _tile(n, want):
    if n <= want:
        return n
    t = want
    while n % t:
        t -= ATTN_BLOCK
    assert t > 0
    return t


def kernel(x, w_in, w_proj_a, w_proj_b, w_out, w_up, w_down, g_pre_mix, g_post_mix, g_pre_mlp,
           g_post_mlp, g_subln, lam_q1, lam_k1, lam_q2, lam_k2):
    batch, seq, d = x.shape
    depth = w_in.shape[0]
    t = batch * seq
    mw = d // 2
    heads_a = mw // HEAD_DIM
    heads_b = mw // DIFF_V_DIM
    assert seq % ATTN_BLOCK == 0 and mw % DIFF_V_DIM == 0
    assert w_in.shape[2] == 6 * mw + 2 * d

    slopes_a = _alibi_slopes(heads_a)
    slopes_b = _alibi_slopes(heads_b)
    tm = _tile(t, 1024)
    tn = _tile(mw, 1024)
    tn_gated = _tile(mw, 512)
    wt = mw // tn
    qk_tile = lambda j: j + jnp.where(j >= 2 * wt, wt, 0)
    gate_tile = lambda j: 6 * wt + j
    plain_tile = lambda j: j

    q_scale = HEAD_DIM ** -0.5 * LOG2E

    def scale_queries(r, tile):
        return r * jnp.where((tile // wt) % 2 == 0, q_scale, 1.0)

    xf = x.reshape(t, d)
    h = _rmsnorm_cast(xf, g_pre_mix[0])
    for l in range(depth):
        qk = _proj(h, w_in, l, qk_tile, 4 * wt, out_dtype=BF16, tm=tm, tn=tn,
                   epilogue=scale_queries, name="qk_proj")
        vt_a = _proj(h, w_in, l, lambda j: 2 * wt + j, wt, out_dtype=BF16, tm=tm, tn=tn,
                     value_rows=HEAD_DIM, name="moba_value_proj")
        vt_b = _proj(h, w_in, l, lambda j: 5 * wt + j, wt, out_dtype=BF16, tm=tm, tn=tn,
                     value_rows=DIFF_V_DIM, name="diff_value_proj")
        gates = _proj(h, w_in, l, gate_tile, 2 * d // tn, out_dtype=F32, tm=tm, tn=tn,
                      name="gate_proj")

        o_a = _moba_attention(qk, vt_a, slopes_a, batch=batch, seq=seq, heads=heads_a,
                              q_col=0, k_col=heads_a)
        lam_init = 0.8 - 0.6 * math.exp(-0.3 * l)
        lam_vecs = jnp.stack([lam_q1[l], lam_k1[l], lam_q2[l], lam_k2[l]]).astype(F32)
        o_b = _diff_attention(qk, vt_b, slopes_b, lam_vecs, g_subln[l], batch=batch, seq=seq,
                              heads=heads_b, q_col=2 * heads_b, k_col=3 * heads_b,
                              lam_init=lam_init)

        y = _gated_proj(o_a, o_b, w_proj_a, w_proj_b, l, gates, tm=tm, tn=tn_gated)
        mix = _proj(y, w_out, l, plain_tile, d // tn, out_dtype=F32, tm=tm, tn=tn, name="out_proj")
        xf, h = _residual(xf, mix, g_post_mix[l], g_pre_mlp[l])

        u, w_down_bf = _proj(h, w_up, l, plain_tile, 4 * d // tn, out_dtype=BF16, tm=tm, tn=tn,
                             epilogue=_relu_squared, also_cast=w_down, name="mlp_up")
        m = _matmul_ksplit(u, w_down_bf, out_dtype=F32, tm=tm, tn=_tile(d, 1024),
                           tk=_tile(4 * d, 4096), name="mlp_down")
        g_next = g_pre_mix[l + 1] if l + 1 < depth else None
        xf, h = _residual(xf, m, g_post_mlp[l], g_next)

    return xf.reshape(batch, seq, d)
```

```python
import functools
import math

import jax
import jax.numpy as jnp
from jax import lax
from jax.experimental import pallas as pl
from jax.experimental.pallas import tpu as pltpu

F32 = jnp.float32
BF16 = jnp.bfloat16

V7X_VMEM_BYTES = 64 * 2**20
V7X_VMEM_COMPILER_RESERVE = 12 * 2**20

HEAD_DIM = 128
DIFF_V_DIM = 256
ATTN_BLOCK = 256
MOBA_TOPK = 3
EPS = 1e-6
NEG = -1e30
LOG2E = math.log2(math.e)

MOBA_HEAD_GROUP = 8
DIFF_HEAD_GROUP = 4
SUM_ROWS = 16

_NT_DIMS = (((1,), (1,)), ((), ()))


def _vmem_limit(block_bytes, scratch_bytes=0):
    want = 2 * block_bytes + scratch_bytes + V7X_VMEM_COMPILER_RESERVE
    return int(min(want, V7X_VMEM_BYTES - 2 * 2**20))


def _nbytes(shape, dtype):
    return math.prod(shape) * jnp.dtype(dtype).itemsize


def _rms(x):
    return x * lax.rsqrt(jnp.mean(x * x, axis=-1, keepdims=True) + EPS)


def _rmsnorm_cast_kernel(x_ref, g_ref, h_ref):
    h_ref[...] = (_rms(x_ref[...]) * g_ref[...]).astype(h_ref.dtype)


def _rmsnorm_cast(x, g, *, rows=256):
    t, d = x.shape
    assert t % rows == 0
    blocks = _nbytes((rows, d), F32) + _nbytes((rows, d), BF16)
    return pl.pallas_call(
        _rmsnorm_cast_kernel,
        grid=(t // rows,),
        in_specs=[pl.BlockSpec((rows, d), lambda i: (i, 0)),
                  pl.BlockSpec((1, d), lambda i: (0, 0))],
        out_specs=pl.BlockSpec((rows, d), lambda i: (i, 0)),
        out_shape=jax.ShapeDtypeStruct((t, d), BF16),
        compiler_params=pltpu.CompilerParams(
            dimension_semantics=("parallel",), vmem_limit_bytes=_vmem_limit(blocks)),
        name="rmsnorm_cast",
    )(x, g.reshape(1, d))


def _residual_kernel(x_ref, m_ref, gpost_ref, gnext_ref, xo_ref, h_ref):
    xn = x_ref[...] + _rms(m_ref[...]) * gpost_ref[...]
    xo_ref[...] = xn
    h_ref[...] = (_rms(xn) * gnext_ref[...]).astype(h_ref.dtype)


def _residual_last_kernel(x_ref, m_ref, gpost_ref, xo_ref):
    xo_ref[...] = x_ref[...] + _rms(m_ref[...]) * gpost_ref[...]


def _residual(x, m, g_post, g_next, *, rows=256):
    t, d = x.shape
    assert t % rows == 0
    row_spec = pl.BlockSpec((rows, d), lambda i: (i, 0))
    gain_spec = pl.BlockSpec((1, d), lambda i: (0, 0))
    blocks = 3 * _nbytes((rows, d), F32) + _nbytes((rows, d), BF16)
    params = pltpu.CompilerParams(
        dimension_semantics=("parallel",), vmem_limit_bytes=_vmem_limit(blocks))
    if g_next is None:
        return pl.pallas_call(
            _residual_last_kernel,
            grid=(t // rows,),
            in_specs=[row_spec, row_spec, gain_spec],
            out_specs=row_spec,
            out_shape=jax.ShapeDtypeStruct((t, d), F32),
            compiler_params=params,
            name="residual_last",
        )(x, m, g_post.reshape(1, d)), None
    return pl.pallas_call(
        _residual_kernel,
        grid=(t // rows,),
        in_specs=[row_spec, row_spec, gain_spec, gain_spec],
        out_specs=[row_spec, row_spec],
        out_shape=[jax.ShapeDtypeStruct((t, d), F32), jax.ShapeDtypeStruct((t, d), BF16)],
        compiler_params=params,
        name="residual_norm",
    )(x, m, g_post.reshape(1, d), g_next.reshape(1, d))


def _identity(x, tile):
    return x


def _relu_squared(x, tile):
    r = jnp.maximum(x, 0.0)
    return r * r


def _cast_weight_once(w_ref, wbf_ref, transpose=False):
    @pl.when(pl.program_id(1) == 0)
    def _():
        w = w_ref[...]
        wbf_ref[...] = (w.T if transpose else w).astype(BF16)


def _proj_kernel(x_ref, w_hbm, o_ref, wstage_ref, wbf_ref, sem, *, layer, col_tile, n_tiles,
                 epilogue, key_blocked_t):
    j = pl.program_id(0)
    k, tn = wstage_ref.shape

    def weight_copy(tile):
        col = pl.multiple_of(col_tile(tile) * tn, tn)
        return pltpu.make_async_copy(w_hbm.at[layer, :, pl.ds(col, tn)], wstage_ref, sem)

    @pl.when(pl.program_id(1) == 0)
    def _():
        @pl.when(j == 0)
        def _():
            weight_copy(j).start()

        weight_copy(j).wait()
        if key_blocked_t:
            for c in range(k // tn):
                rows = slice(c * tn, (c + 1) * tn)
                wbf_ref[:, rows] = wstage_ref[rows, :].T.astype(BF16)
        else:
            wbf_ref[...] = wstage_ref[...].astype(BF16)

        @pl.when(j + 1 < n_tiles)
        def _():
            weight_copy(j + 1).start()

    if key_blocked_t:
        rt = epilogue(lax.dot_general(wbf_ref[...], x_ref[...], _NT_DIMS,
                                      preferred_element_type=F32), j)
        n_chunks, n_heads, rows, _ = o_ref.shape
        rows -= SUM_ROWS
        ones = jnp.ones((SUM_ROWS, ATTN_BLOCK), o_ref.dtype)
        for c in range(n_chunks):
            for hh in range(n_heads):
                o_ref[c, hh, :rows, :] = rt[hh * rows:(hh + 1) * rows,
                                            c * ATTN_BLOCK:(c + 1) * ATTN_BLOCK].astype(o_ref.dtype)
                o_ref[c, hh, rows:, :] = ones
    else:
        r = epilogue(jnp.dot(x_ref[...], wbf_ref[...], preferred_element_type=F32), j)
        o_ref[...] = r.astype(o_ref.dtype)


def _proj_and_cast_kernel(x_ref, w_hbm, side_ref, o_ref, side_bf_ref, *scratch, **static):
    side_bf_ref[...] = side_ref[...].astype(side_bf_ref.dtype)
    _proj_kernel(x_ref, w_hbm, o_ref, *scratch, **static)


def _proj(x, w, layer, col_tile, n_tiles, *, out_dtype, tm, tn, epilogue=_identity,
          value_rows=None, also_cast=None, name):
    m, k = x.shape
    assert m % tm == 0 and w.shape[1] == k and w.shape[2] % tn == 0 and k % tn == 0
    n_m = m // tm
    key_blocked_t = value_rows is not None
    if key_blocked_t:
        assert tm % ATTN_BLOCK == 0 and tn % value_rows == 0
        hpt = tn // value_rows
        block = (tm // ATTN_BLOCK, hpt, value_rows + SUM_ROWS, ATTN_BLOCK)
        out_shape = jax.ShapeDtypeStruct((m // ATTN_BLOCK, n_tiles * hpt) + block[2:], out_dtype)
        out_spec = pl.BlockSpec(block, lambda j, i: (i, j, 0, 0))
    else:
        block = (tm, tn)
        out_shape = jax.ShapeDtypeStruct((m, n_tiles * tn), out_dtype)
        out_spec = pl.BlockSpec(block, lambda j, i: (i, j))
    blocks = _nbytes((tm, k), BF16) + _nbytes(block, out_dtype)
    scratch_bytes = _nbytes((k, tn), F32) + _nbytes((k, tn), BF16)
    body = _proj_kernel
    operands = [x, w]
    in_specs = [pl.BlockSpec((tm, k), lambda j, i: (i, 0)), pl.BlockSpec(memory_space=pl.ANY)]
    if also_cast is not None:
        _, side_k, side_n = also_cast.shape
        assert side_k % (n_tiles * n_m) == 0
        slab = (side_k // (n_tiles * n_m), side_n)
        body = _proj_and_cast_kernel
        operands.append(also_cast)
        in_specs.append(pl.BlockSpec((None,) + slab, lambda j, i: (layer, j * n_m + i, 0)))
        out_spec = [out_spec, pl.BlockSpec(slab, lambda j, i: (j * n_m + i, 0))]
        out_shape = [out_shape, jax.ShapeDtypeStruct((side_k, side_n), BF16)]
        blocks += _nbytes(slab, F32) + _nbytes(slab, BF16)
    return pl.pallas_call(
        functools.partial(body, layer=layer, col_tile=col_tile, n_tiles=n_tiles,
                          epilogue=epilogue, key_blocked_t=key_blocked_t),
        grid=(n_tiles, n_m),
        in_specs=in_specs,
        out_specs=out_spec,
        out_shape=out_shape,
        scratch_shapes=[pltpu.VMEM((k, tn), F32),
                        pltpu.VMEM((tn, k) if key_blocked_t else (k, tn), BF16),
                        pltpu.SemaphoreType.DMA(())],
        compiler_params=pltpu.CompilerParams(
            dimension_semantics=("arbitrary", "arbitrary"),
            vmem_limit_bytes=_vmem_limit(blocks, scratch_bytes)),
        name=name,
    )(*operands)


def _matmul_ksplit_kernel(x_ref, w_ref, o_ref, acc_ref, *, n_k):
    k = pl.program_id(2)

    @pl.when(k == 0)
    def _():
        acc_ref[...] = jnp.zeros_like(acc_ref)

    acc_ref[...] += jnp.dot(x_ref[...], w_ref[...], preferred_element_type=F32)

    @pl.when(k == n_k - 1)
    def _():
        o_ref[...] = acc_ref[...].astype(o_ref.dtype)


def _matmul_ksplit(x, w, *, out_dtype, tm, tn, tk, name):
    m, k = x.shape
    _, n = w.shape
    assert m % tm == 0 and n % tn == 0 and k % tk == 0
    n_k = k // tk
    blocks = _nbytes((tm, tk), BF16) + _nbytes((tk, tn), BF16) + _nbytes((tm, tn), out_dtype)
    return pl.pallas_call(
        functools.partial(_matmul_ksplit_kernel, n_k=n_k),
        grid=(m // tm, n // tn, n_k),
        in_specs=[pl.BlockSpec((tm, tk), lambda i, j, kk: (i, kk)),
                  pl.BlockSpec((tk, tn), lambda i, j, kk: (kk, j))],
        out_specs=pl.BlockSpec((tm, tn), lambda i, j, kk: (i, j)),
        out_shape=jax.ShapeDtypeStruct((m, n), out_dtype),
        scratch_shapes=[pltpu.VMEM((tm, tn), F32)],
        compiler_params=pltpu.CompilerParams(
            dimension_semantics=("parallel", "parallel", "arbitrary"),
            vmem_limit_bytes=_vmem_limit(blocks, _nbytes((tm, tn), F32))),
        name=name,
    )(x, w)


def _gated_proj_kernel(oa_ref, ob_ref, wa_ref, wb_ref, ga_ref, gb_ref, y_ref, wabf_ref, wbbf_ref):
    _cast_weight_once(wa_ref, wabf_ref)
    _cast_weight_once(wb_ref, wbbf_ref)
    pa = jnp.dot(oa_ref[...], wabf_ref[...], preferred_element_type=F32)
    pb = jnp.dot(ob_ref[...], wbbf_ref[...], preferred_element_type=F32)
    y = jax.nn.sigmoid(ga_ref[...]) * pa + jax.nn.sigmoid(gb_ref[...]) * pb
    y_ref[...] = y.astype(y_ref.dtype)


def _gated_proj(o_a, o_b, w_a, w_b, layer, gates, *, tm, tn):
    t, ka = o_a.shape
    _, kb = o_b.shape
    d = w_a.shape[2]
    assert t % tm == 0 and d % tn == 0
    n_n = d // tn
    blocks = (_nbytes((tm, ka), BF16) + _nbytes((tm, kb), BF16) + _nbytes((ka, tn), F32)
              + _nbytes((kb, tn), F32) + 2 * _nbytes((tm, tn), F32) + _nbytes((tm, tn), BF16))
    scratch = _nbytes((ka, tn), BF16) + _nbytes((kb, tn), BF16)
    return pl.pallas_call(
        _gated_proj_kernel,
        grid=(n_n, t // tm),
        in_specs=[pl.BlockSpec((tm, ka), lambda j, i: (i, 0)),
                  pl.BlockSpec((tm, kb), lambda j, i: (i, 0)),
                  pl.BlockSpec((None, ka, tn), lambda j, i: (layer, 0, j)),
                  pl.BlockSpec((None, kb, tn), lambda j, i: (layer, 0, j)),
                  pl.BlockSpec((tm, tn), lambda j, i: (i, j)),
                  pl.BlockSpec((tm, tn), lambda j, i: (i, n_n + j))],
        out_specs=pl.BlockSpec((tm, tn), lambda j, i: (i, j)),
        out_shape=jax.ShapeDtypeStruct((t, d), BF16),
        scratch_shapes=[pltpu.VMEM((ka, tn), BF16), pltpu.VMEM((kb, tn), BF16)],
        compiler_params=pltpu.CompilerParams(
            dimension_semantics=("parallel", "arbitrary"),
            vmem_limit_bytes=_vmem_limit(blocks, scratch)),
        name="gated_proj",
    )(o_a, o_b, w_a, w_b, gates, gates)


def _tile_indices():
    kk = lax.broadcasted_iota(jnp.int32, (ATTN_BLOCK, ATTN_BLOCK), 0)
    qq = lax.broadcasted_iota(jnp.int32, (ATTN_BLOCK, ATTN_BLOCK), 1)
    return kk, qq


def _key_block(k_ref, j, g):
    rows = pl.ds(pl.multiple_of(j * ATTN_BLOCK, ATTN_BLOCK), ATTN_BLOCK)
    return k_ref[rows, g * HEAD_DIM:(g + 1) * HEAD_DIM]


def _store_alibi_tiles(alibi_ref, slopes2):
    kk, qq = _tile_indices()
    dist = (kk - qq).astype(F32)
    for g, slope2 in enumerate(slopes2):
        alibi_ref[0, g] = slope2 * dist
        alibi_ref[1, g] = jnp.where(kk <= qq, slope2 * dist, NEG)


def _flash_blocks(i, n_units, *, scores, values, alibi, bias, acc_ref):
    units = range(n_units)

    def sweep(rs, ms):
        ms = list(ms)
        blocks = [jnp.where(r == 0, i, r - 1) for r in rs]
        owns = [jnp.where(r == 0, 1, 0) for r in rs]
        ss = [[scores(j, u) for u in units] for j in blocks]
        ps, alphas = [], []
        for n, j in enumerate(blocks):
            ps.append([])
            alphas.append([])
            for u in units:
                t = ss[n][u] + alibi(u, owns[n])
                b = bias(j, u)
                m_new = jnp.maximum(ms[u], jnp.max(t, axis=0, keepdims=True) + b)
                ps[n].append(jnp.exp2(t - (m_new - b)).astype(BF16))
                alphas[n].append(jnp.exp2(ms[u] - m_new))
                ms[u] = m_new
        for u in units:
            acc = acc_ref[u]
            for n, j in enumerate(blocks):
                acc = alphas[n][u] * acc + jnp.dot(values(j, u), ps[n][u],
                                                   preferred_element_type=F32)
            acc_ref[u] = acc
        return tuple(ms)

    for u in units:
        acc_ref[u] = jnp.zeros(acc_ref.shape[1:], acc_ref.dtype)
    ms = tuple(jnp.full((1, ATTN_BLOCK), NEG, F32) for _ in units)

    n_blocks = i + 1
    ms = lax.fori_loop(0, n_blocks // 2, lambda k, c: sweep([2 * k, 2 * k + 1], c), ms)

    @pl.when(lax.rem(n_blocks, 2) == 1)
    def _():
        sweep([i], ms)


def _moba_kernel(slopes_ref, q_ref, k_ref, vt_ref, o_ref, kmean_ref, alibi_ref, bias_ref, acc_ref,
                 *, group):
    n_blocks = kmean_ref.shape[0]
    seq = q_ref.shape[0]
    hg = pl.program_id(1)
    i = pl.program_id(2)

    @pl.when(i == 0)
    def _():
        slopes2 = [slopes_ref[hg * group + g] * LOG2E for g in range(group)]
        for j in range(n_blocks):
            kj = k_ref[j * ATTN_BLOCK:(j + 1) * ATTN_BLOCK, :].astype(F32)
            kmean_ref[j:j + 1, :] = jnp.sum(kj, axis=0, keepdims=True) * (1.0 / ATTN_BLOCK)
        _store_alibi_tiles(alibi_ref, slopes2)

        row = lax.broadcasted_iota(jnp.int32, (n_blocks, seq), 0)
        own = lax.broadcasted_iota(jnp.int32, (n_blocks, seq), 1) // ATTN_BLOCK
        for g in range(group):
            cols = slice(g * HEAD_DIM, (g + 1) * HEAD_DIM)
            terms, rest = [], kmean_ref[:, cols]
            for _ in range(3):
                terms.append(rest.astype(BF16).astype(F32))
                rest = rest - terms[-1]
            terms.append(jnp.zeros((HEAD_DIM - 3 * n_blocks, HEAD_DIM), F32))
            stacked = jnp.concatenate(terms, axis=0).astype(BF16)
            parts = lax.dot_general(q_ref[:, cols], stacked, _NT_DIMS,
                                    preferred_element_type=F32).T
            gate = (parts[:n_blocks] + parts[n_blocks:2 * n_blocks]
                    + parts[2 * n_blocks:3 * n_blocks])
            rank = jnp.zeros(gate.shape, jnp.int32)
            for jp in range(n_blocks):
                g_jp = gate[jp:jp + 1, :]
                beats = ((g_jp > gate) | ((g_jp == gate) & (jp < row))) & (jp < own)
                rank = rank + jnp.where(beats, 1, 0)
            selected = (row < own) & (rank < MOBA_TOPK)
            block_off = slopes2[g] * ((row - own) * ATTN_BLOCK).astype(F32)
            bias = jnp.where(selected | (row == own), block_off, NEG)
            for qi in range(n_blocks):
                bias_ref[g, qi] = bias[:, qi * ATTN_BLOCK:(qi + 1) * ATTN_BLOCK]

    q_rows = pl.ds(pl.multiple_of(i * ATTN_BLOCK, ATTN_BLOCK), ATTN_BLOCK)
    qs = [q_ref[q_rows, g * HEAD_DIM:(g + 1) * HEAD_DIM] for g in range(group)]

    _flash_blocks(
        i, group,
        scores=lambda j, g: lax.dot_general(_key_block(k_ref, j, g), qs[g], _NT_DIMS,
                                            preferred_element_type=F32),
        values=lambda j, g: vt_ref[j, g],
        alibi=lambda g, own: alibi_ref[own, g],
        bias=lambda j, g: bias_ref[g, i, pl.ds(j, 1), :],
        acc_ref=acc_ref)
    for g in range(group):
        o = acc_ref[g, :HEAD_DIM, :] * (1.0 / acc_ref[g, HEAD_DIM:HEAD_DIM + 1, :])
        o_ref[:, g * HEAD_DIM:(g + 1) * HEAD_DIM] = o.T.astype(o_ref.dtype)


def _moba_attention(qk, vt, slopes, *, batch, seq, heads, q_col, k_col):
    nb = seq // ATTN_BLOCK
    t = batch * seq
    group = math.gcd(heads, MOBA_HEAD_GROUP)
    gw = group * HEAD_DIM
    v_rows = HEAD_DIM + SUM_ROWS
    assert q_col % group == 0 and k_col % group == 0 and vt.shape[1:] == (heads, v_rows, ATTN_BLOCK)
    blocks = (_nbytes((ATTN_BLOCK, gw), BF16) + 2 * _nbytes((seq, gw), BF16)
              + _nbytes((nb, group, v_rows, ATTN_BLOCK), BF16))
    scratch_types = [((nb, gw), F32), ((2, group, ATTN_BLOCK, ATTN_BLOCK), F32),
                     ((group, nb, nb, ATTN_BLOCK), F32), ((group, v_rows, ATTN_BLOCK), F32)]
    scratch = [pltpu.VMEM(shape, dtype) for shape, dtype in scratch_types]
    scratch_bytes = sum(_nbytes(shape, dtype) for shape, dtype in scratch_types)
    return pl.pallas_call(
        functools.partial(_moba_kernel, group=group),
        grid=(batch, heads // group, nb),
        in_specs=[pl.BlockSpec(memory_space=pltpu.SMEM),
                  pl.BlockSpec((seq, gw), lambda b, h, i: (b, q_col // group + h)),
                  pl.BlockSpec((seq, gw), lambda b, h, i: (b, k_col // group + h)),
                  pl.BlockSpec((nb, group, v_rows, ATTN_BLOCK), lambda b, h, i: (b, h, 0, 0))],
        out_specs=pl.BlockSpec((ATTN_BLOCK, gw), lambda b, h, i: (b * nb + i, h)),
        out_shape=jax.ShapeDtypeStruct((t, heads * HEAD_DIM), BF16),
        scratch_shapes=scratch,
        compiler_params=pltpu.CompilerParams(
            dimension_semantics=("parallel", "parallel", "arbitrary"),
            vmem_limit_bytes=_vmem_limit(blocks, scratch_bytes)),
        name="moba_attention",
    )(slopes, qk, qk, vt)


def _diff_kernel(slopes_ref, lamv_ref, g_ref, q_ref, k_ref, vt_ref, o_ref, alibi_ref, acc_ref,
                 *, lam_init, group):
    hg = pl.program_id(1)
    i = pl.program_id(2)
    slopes2 = [slopes_ref[hg * group + g] * LOG2E for g in range(group)]

    @pl.when(i == 0)
    def _():
        _store_alibi_tiles(alibi_ref, slopes2)

    qs = [q_ref[:, c * HEAD_DIM:(c + 1) * HEAD_DIM] for c in range(2 * group)]

    _flash_blocks(
        i, 2 * group,
        scores=lambda j, c: lax.dot_general(_key_block(k_ref, j, c), qs[c], _NT_DIMS,
                                            preferred_element_type=F32),
        values=lambda j, c: vt_ref[j, c // 2],
        alibi=lambda c, own: alibi_ref[own, c // 2],
        bias=lambda j, c: slopes2[c // 2] * ((j - i) * ATTN_BLOCK).astype(F32),
        acc_ref=acc_ref)

    lv = lamv_ref[...]
    lam = (jnp.exp(jnp.sum(lv[0:1] * lv[1:2], axis=-1, keepdims=True))
           - jnp.exp(jnp.sum(lv[2:3] * lv[3:4], axis=-1, keepdims=True)) + lam_init)

    def normalised(c):
        return acc_ref[c, :DIFF_V_DIM, :] * (1.0 / acc_ref[c, DIFF_V_DIM:DIFF_V_DIM + 1, :])

    gain = g_ref[...] * (1.0 - lam_init)
    for g in range(group):
        o = normalised(2 * g) - lam * normalised(2 * g + 1)
        o = o * lax.rsqrt(jnp.mean(o * o, axis=0, keepdims=True) + EPS)
        o_ref[:, g * DIFF_V_DIM:(g + 1) * DIFF_V_DIM] = (o.T * gain).astype(o_ref.dtype)


def _diff_attention(qk, vt, slopes, lam_vecs, g_subln, *, batch, seq, heads, q_col, k_col,
                    lam_init):
    nb = seq // ATTN_BLOCK
    t = batch * seq
    group = math.gcd(heads, DIFF_HEAD_GROUP)
    gw = group * DIFF_V_DIM
    v_rows = DIFF_V_DIM + SUM_ROWS
    assert q_col % group == 0 and k_col % group == 0 and vt.shape[1:] == (heads, v_rows, ATTN_BLOCK)
    blocks = (_nbytes((ATTN_BLOCK, gw), BF16) + _nbytes((seq, gw), BF16)
              + _nbytes((nb, group, v_rows, ATTN_BLOCK), BF16) + _nbytes((ATTN_BLOCK, gw), BF16))
    scratch_types = [((2, group, ATTN_BLOCK, ATTN_BLOCK), F32),
                     ((2 * group, v_rows, ATTN_BLOCK), F32)]
    scratch_bytes = sum(_nbytes(shape, dtype) for shape, dtype in scratch_types)
    return pl.pallas_call(
        functools.partial(_diff_kernel, lam_init=lam_init, group=group),
        grid=(batch, heads // group, nb),
        in_specs=[pl.BlockSpec(memory_space=pltpu.SMEM),
                  pl.BlockSpec((4, HEAD_DIM), lambda b, h, i: (0, 0)),
                  pl.BlockSpec((1, DIFF_V_DIM), lambda b, h, i: (0, 0)),
                  pl.BlockSpec((ATTN_BLOCK, gw), lambda b, h, i: (b * nb + i, q_col // group + h)),
                  pl.BlockSpec((seq, gw), lambda b, h, i: (b, k_col // group + h)),
                  pl.BlockSpec((nb, group, v_rows, ATTN_BLOCK), lambda b, h, i: (b, h, 0, 0))],
        out_specs=pl.BlockSpec((ATTN_BLOCK, gw), lambda b, h, i: (b * nb + i, h)),
        out_shape=jax.ShapeDtypeStruct((t, heads * DIFF_V_DIM), BF16),
        scratch_shapes=[pltpu.VMEM(shape, dtype) for shape, dtype in scratch_types],
        compiler_params=pltpu.CompilerParams(
            dimension_semantics=("parallel", "parallel", "arbitrary"),
            vmem_limit_bytes=_vmem_limit(blocks, scratch_bytes)),
        name="diff_attention",
    )(slopes, lam_vecs, g_subln.reshape(1, DIFF_V_DIM), qk, qk, vt)


def _alibi_slopes(n_heads):
    return 2.0 ** (-8.0 * jnp.arange(1, n_heads + 1, dtype=F32) / n_heads)


def _tile(n, want):
    if n <= want:
        return n
    t = want
    while n % t:
        t -= ATTN_BLOCK
    assert t > 0
    return t


def kernel(x, w_in, w_proj_a, w_proj_b, w_out, w_up, w_down, g_pre_mix, g_post_mix, g_pre_mlp,
           g_post_mlp, g_subln, lam_q1, lam_k1, lam_q2, lam_k2):
    batch, seq, d = x.shape
    depth = w_in.shape[0]
    t = batch * seq
    mw = d // 2
    heads_a = mw // HEAD_DIM
    heads_b = mw // DIFF_V_DIM
    assert seq % ATTN_BLOCK == 0 and mw % DIFF_V_DIM == 0
    assert w_in.shape[2] == 6 * mw + 2 * d

    slopes_a = _alibi_slopes(heads_a)
    slopes_b = _alibi_slopes(heads_b)
    tm = _tile(t, 1024)
    tn = _tile(mw, 1024)
    tn_gated = _tile(mw, 512)
    wt = mw // tn
    qk_tile = lambda j: j + jnp.where(j >= 2 * wt, wt, 0)
    gate_tile = lambda j: 6 * wt + j
    plain_tile = lambda j: j

    q_scale = HEAD_DIM ** -0.5 * LOG2E

    def scale_queries(r, tile):
        return r * jnp.where((tile // wt) % 2 == 0, q_scale, 1.0)

    xf = x.reshape(t, d)
    h = _rmsnorm_cast(xf, g_pre_mix[0])
    for l in range(depth):
        qk = _proj(h, w_in, l, qk_tile, 4 * wt, out_dtype=BF16, tm=tm, tn=tn,
                   epilogue=scale_queries, name="qk_proj")
        vt_a = _proj(h, w_in, l, lambda j: 2 * wt + j, wt, out_dtype=BF16, tm=tm, tn=tn,
                     value_rows=HEAD_DIM, name="moba_value_proj")
        vt_b = _proj(h, w_in, l, lambda j: 5 * wt + j, wt, out_dtype=BF16, tm=tm, tn=tn,
                     value_rows=DIFF_V_DIM, name="diff_value_proj")
        gates = _proj(h, w_in, l, gate_tile, 2 * d // tn, out_dtype=F32, tm=tm, tn=tn,
                      name="gate_proj")

        o_a = _moba_attention(qk, vt_a, slopes_a, batch=batch, seq=seq, heads=heads_a,
                              q_col=0, k_col=heads_a)
        lam_init = 0.8 - 0.6 * math.exp(-0.3 * l)
        lam_vecs = jnp.stack([lam_q1[l], lam_k1[l], lam_q2[l], lam_k2[l]]).astype(F32)
        o_b = _diff_attention(qk, vt_b, slopes_b, lam_vecs, g_subln[l], batch=batch, seq=seq,
                              heads=heads_b, q_col=2 * heads_b, k_col=3 * heads_b,
                              lam_init=lam_init)

        y = _gated_proj(o_a, o_b, w_proj_a, w_proj_b, l, gates, tm=tm, tn=tn_gated)
        mix = _proj(y, w_out, l, plain_tile, d // tn, out_dtype=F32, tm=tm, tn=tn, name="out_proj")
        xf, h = _residual(xf, mix, g_post_mix[l], g_pre_mlp[l])

        u, w_down_bf = _proj(h, w_up, l, plain_tile, 4 * d // tn, out_dtype=BF16, tm=tm, tn=tn,
                             epilogue=_relu_squared, also_cast=w_down, name="mlp_up")
        m = _matmul_ksplit(u, w_down_bf, out_dtype=F32, tm=tm, tn=_tile(d, 1024),
                           tk=_tile(4 * d, 4096), name="mlp_down")
        g_next = g_pre_mix[l + 1] if l + 1 < depth else None
        xf, h = _residual(xf, m, g_post_mlp[l], g_next)

    return xf.reshape(batch, seq, d)
```

```python
import functools
import math

import jax
import jax.numpy as jnp
from jax import lax
from jax.experimental import pallas as pl
from jax.experimental.pallas import tpu as pltpu

F32 = jnp.float32
BF16 = jnp.bfloat16

V7X_VMEM_BYTES = 64 * 2**20
V7X_VMEM_COMPILER_RESERVE = 12 * 2**20

HEAD_DIM = 128
DIFF_V_DIM = 256
ATTN_BLOCK = 256
MOBA_TOPK = 3
EPS = 1e-6
NEG = -1e30
LOG2E = math.log2(math.e)

MOBA_HEAD_GROUP = 8
DIFF_HEAD_GROUP = 4
SUM_ROWS = 16

_NT_DIMS = (((1,), (1,)), ((), ()))


def _vmem_limit(block_bytes, scratch_bytes=0):
    want = 2 * block_bytes + scratch_bytes + V7X_VMEM_COMPILER_RESERVE
    return int(min(want, V7X_VMEM_BYTES - 2 * 2**20))


def _nbytes(shape, dtype):
    return math.prod(shape) * jnp.dtype(dtype).itemsize


def _rms(x):
    return x * lax.rsqrt(jnp.mean(x * x, axis=-1, keepdims=True) + EPS)


def _rmsnorm_cast_kernel(x_ref, g_ref, h_ref):
    h_ref[...] = (_rms(x_ref[...]) * g_ref[...]).astype(h_ref.dtype)


def _rmsnorm_cast(x, g, *, rows=256):
    t, d = x.shape
    assert t % rows == 0
    blocks = _nbytes((rows, d), F32) + _nbytes((rows, d), BF16)
    return pl.pallas_call(
        _rmsnorm_cast_kernel,
        grid=(t // rows,),
        in_specs=[pl.BlockSpec((rows, d), lambda i: (i, 0)),
                  pl.BlockSpec((1, d), lambda i: (0, 0))],
        out_specs=pl.BlockSpec((rows, d), lambda i: (i, 0)),
        out_shape=jax.ShapeDtypeStruct((t, d), BF16),
        compiler_params=pltpu.CompilerParams(
            dimension_semantics=("parallel",), vmem_limit_bytes=_vmem_limit(blocks)),
        name="rmsnorm_cast",
    )(x, g.reshape(1, d))


def _residual_kernel(x_ref, m_ref, gpost_ref, gnext_ref, xo_ref, h_ref):
    xn = x_ref[...] + _rms(m_ref[...]) * gpost_ref[...]
    xo_ref[...] = xn
    h_ref[...] = (_rms(xn) * gnext_ref[...]).astype(h_ref.dtype)


def _residual_last_kernel(x_ref, m_ref, gpost_ref, xo_ref):
    xo_ref[...] = x_ref[...] + _rms(m_ref[...]) * gpost_ref[...]


def _residual(x, m, g_post, g_next, *, rows=256):
    t, d = x.shape
    assert t % rows == 0
    row_spec = pl.BlockSpec((rows, d), lambda i: (i, 0))
    gain_spec = pl.BlockSpec((1, d), lambda i: (0, 0))
    blocks = 3 * _nbytes((rows, d), F32) + _nbytes((rows, d), BF16)
    params = pltpu.CompilerParams(
        dimension_semantics=("parallel",), vmem_limit_bytes=_vmem_limit(blocks))
    if g_next is None:
        return pl.pallas_call(
            _residual_last_kernel,
            grid=(t // rows,),
            in_specs=[row_spec, row_spec, gain_spec],
            out_specs=row_spec,
            out_shape=jax.ShapeDtypeStruct((t, d), F32),
            compiler_params=params,
            name="residual_last",
        )(x, m, g_post.reshape(1, d)), None
    return pl.pallas_call(
        _residual_kernel,
        grid=(t // rows,),
        in_specs=[row_spec, row_spec, gain_spec, gain_spec],
        out_specs=[row_spec, row_spec],
        out_shape=[jax.ShapeDtypeStruct((t, d), F32), jax.ShapeDtypeStruct((t, d), BF16)],
        compiler_params=params,
        name="residual_norm",
    )(x, m, g_post.reshape(1, d), g_next.reshape(1, d))


def _identity(x, tile):
    return x


def _relu_squared(x, tile):
    r = jnp.maximum(x, 0.0)
    return r * r


def _proj_kernel(x_ref, w_hbm, o_ref, wstage_ref, wbf_ref, sem, *, layer, col_tile, n_tiles,
                 epilogue, key_blocked_t):
    j = pl.program_id(0)
    k, tn = wstage_ref.shape

    def weight_copy(tile):
        col = pl.multiple_of(col_tile(tile) * tn, tn)
        return pltpu.make_async_copy(w_hbm.at[layer, :, pl.ds(col, tn)], wstage_ref, sem)

    def product(round_weights):
        slabs = [slice(c * tn, (c + 1) * tn) for c in range(k // tn)] if round_weights else [
            slice(None)]
        acc = None
        for rows in slabs:
            if key_blocked_t:
                if round_weights:
                    wbf_ref[:, rows] = wstage_ref[rows, :].T.astype(BF16)
                part = lax.dot_general(wbf_ref[:, rows], x_ref[:, rows], _NT_DIMS,
                                       preferred_element_type=F32)
            else:
                if round_weights:
                    wbf_ref[rows, :] = wstage_ref[rows, :].astype(BF16)
                part = jnp.dot(x_ref[:, rows], wbf_ref[rows, :], preferred_element_type=F32)
            acc = part if acc is None else acc + part
        return acc

    def store(r):
        r = epilogue(r, j)
        if key_blocked_t:
            n_chunks, n_heads, rows, _ = o_ref.shape
            rows -= SUM_ROWS
            ones = jnp.ones((SUM_ROWS, ATTN_BLOCK), o_ref.dtype)
            for c in range(n_chunks):
                for hh in range(n_heads):
                    o_ref[c, hh, :rows, :] = r[hh * rows:(hh + 1) * rows,
                                               c * ATTN_BLOCK:(c + 1) * ATTN_BLOCK].astype(o_ref.dtype)
                    o_ref[c, hh, rows:, :] = ones
        else:
            o_ref[...] = r.astype(o_ref.dtype)

    first_token_step = pl.program_id(1) == 0

    @pl.when(first_token_step)
    def _():
        @pl.when(j == 0)
        def _():
            weight_copy(j).start()

        weight_copy(j).wait()
        r = product(round_weights=True)

        @pl.when(j + 1 < n_tiles)
        def _():
            weight_copy(j + 1).start()

        store(r)

    @pl.when(jnp.logical_not(first_token_step))
    def _():
        store(product(round_weights=False))


def _proj_and_cast_kernel(x_ref, w_hbm, side_ref, o_ref, side_bf_ref, *scratch, **static):
    side_bf_ref[...] = side_ref[...].astype(side_bf_ref.dtype)
    _proj_kernel(x_ref, w_hbm, o_ref, *scratch, **static)


def _proj(x, w, layer, col_tile, n_tiles, *, out_dtype, tm, tn, epilogue=_identity,
          value_rows=None, also_cast=None, name):
    m, k = x.shape
    assert m % tm == 0 and w.shape[1] == k and w.shape[2] % tn == 0 and k % tn == 0
    n_m = m // tm
    key_blocked_t = value_rows is not None
    if key_blocked_t:
        assert tm % ATTN_BLOCK == 0 and tn % value_rows == 0
        hpt = tn // value_rows
        block = (tm // ATTN_BLOCK, hpt, value_rows + SUM_ROWS, ATTN_BLOCK)
        out_shape = jax.ShapeDtypeStruct((m // ATTN_BLOCK, n_tiles * hpt) + block[2:], out_dtype)
        out_spec = pl.BlockSpec(block, lambda j, i: (i, j, 0, 0))
    else:
        block = (tm, tn)
        out_shape = jax.ShapeDtypeStruct((m, n_tiles * tn), out_dtype)
        out_spec = pl.BlockSpec(block, lambda j, i: (i, j))
    blocks = _nbytes((tm, k), BF16) + _nbytes(block, out_dtype)
    scratch_bytes = _nbytes((k, tn), F32) + _nbytes((k, tn), BF16)
    body = _proj_kernel
    operands = [x, w]
    in_specs = [pl.BlockSpec((tm, k), lambda j, i: (i, 0)), pl.BlockSpec(memory_space=pl.ANY)]
    if also_cast is not None:
        _, side_k, side_n = also_cast.shape
        assert side_k % (n_tiles * n_m) == 0
        slab = (side_k // (n_tiles * n_m), side_n)
        body = _proj_and_cast_kernel
        operands.append(also_cast)
        in_specs.append(pl.BlockSpec((None,) + slab, lambda j, i: (layer, j * n_m + i, 0)))
        out_spec = [out_spec, pl.BlockSpec(slab, lambda j, i: (j * n_m + i, 0))]
        out_shape = [out_shape, jax.ShapeDtypeStruct((side_k, side_n), BF16)]
        blocks += _nbytes(slab, F32) + _nbytes(slab, BF16)
    return pl.pallas_call(
        functools.partial(body, layer=layer, col_tile=col_tile, n_tiles=n_tiles,
                          epilogue=epilogue, key_blocked_t=key_blocked_t),
        grid=(n_tiles, n_m),
        in_specs=in_specs,
        out_specs=out_spec,
        out_shape=out_shape,
        scratch_shapes=[pltpu.VMEM((k, tn), F32),
                        pltpu.VMEM((tn, k) if key_blocked_t else (k, tn), BF16),
                        pltpu.SemaphoreType.DMA(())],
        compiler_params=pltpu.CompilerParams(
            dimension_semantics=("arbitrary", "arbitrary"),
            vmem_limit_bytes=_vmem_limit(blocks, scratch_bytes)),
        name=name,
    )(*operands)


def _matmul_ksplit_kernel(x_ref, w_ref, o_ref, acc_ref, *, n_k):
    k = pl.program_id(2)

    @pl.when(k == 0)
    def _():
        acc_ref[...] = jnp.zeros_like(acc_ref)

    acc_ref[...] += jnp.dot(x_ref[...], w_ref[...], preferred_element_type=F32)

    @pl.when(k == n_k - 1)
    def _():
        o_ref[...] = acc_ref[...].astype(o_ref.dtype)


def _matmul_ksplit(x, w, *, out_dtype, tm, tn, tk, name):
    m, k = x.shape
    _, n = w.shape
    assert m % tm == 0 and n % tn == 0 and k % tk == 0
    n_k = k // tk
    blocks = _nbytes((tm, tk), BF16) + _nbytes((tk, tn), BF16) + _nbytes((tm, tn), out_dtype)
    return pl.pallas_call(
        functools.partial(_matmul_ksplit_kernel, n_k=n_k),
        grid=(m // tm, n // tn, n_k),
        in_specs=[pl.BlockSpec((tm, tk), lambda i, j, kk: (i, kk)),
                  pl.BlockSpec((tk, tn), lambda i, j, kk: (kk, j))],
        out_specs=pl.BlockSpec((tm, tn), lambda i, j, kk: (i, j)),
        out_shape=jax.ShapeDtypeStruct((m, n), out_dtype),
        scratch_shapes=[pltpu.VMEM((tm, tn), F32)],
        compiler_params=pltpu.CompilerParams(
            dimension_semantics=("parallel", "parallel", "arbitrary"),
            vmem_limit_bytes=_vmem_limit(blocks, _nbytes((tm, tn), F32))),
        name=name,
    )(x, w)


def _sigmoid(x):
    return 0.5 * jnp.tanh(0.5 * x) + 0.5


def _gated_proj_kernel(oa_ref, ob_ref, wa_ref, wb_ref, ga_ref, gb_ref, y_ref, wabf_ref, wbbf_ref,
                       *, n_slabs):
    def product(x_ref, w_ref, wbf_ref, round_weights):
        k = w_ref.shape[0]
        slabs = ([slice(c * k // n_slabs, (c + 1) * k // n_slabs) for c in range(n_slabs)]
                 if round_weights else [slice(None)])
        acc = None
        for rows in slabs:
            if round_weights:
                wbf_ref[rows, :] = w_ref[rows, :].astype(BF16)
            part = jnp.dot(x_ref[:, rows], wbf_ref[rows, :], preferred_element_type=F32)
            acc = part if acc is None else acc + part
        return acc

    def merge(round_weights):
        pa = product(oa_ref, wa_ref, wabf_ref, round_weights)
        pb = product(ob_ref, wb_ref, wbbf_ref, round_weights)
        y = _sigmoid(ga_ref[...]) * pa + _sigmoid(gb_ref[...]) * pb
        y_ref[...] = y.astype(y_ref.dtype)

    first_token_step = pl.program_id(1) == 0
    pl.when(first_token_step)(functools.partial(merge, True))
    pl.when(jnp.logical_not(first_token_step))(functools.partial(merge, False))


def _gated_proj(o_a, o_b, w_a, w_b, layer, gates, *, tm, tn):
    t, ka = o_a.shape
    _, kb = o_b.shape
    d = w_a.shape[2]
    assert t % tm == 0 and d % tn == 0
    n_n = d // tn
    blocks = (_nbytes((tm, ka), BF16) + _nbytes((tm, kb), BF16) + _nbytes((ka, tn), F32)
              + _nbytes((kb, tn), F32) + 2 * _nbytes((tm, tn), F32) + _nbytes((tm, tn), BF16))
    scratch = _nbytes((ka, tn), BF16) + _nbytes((kb, tn), BF16)
    return pl.pallas_call(
        functools.partial(_gated_proj_kernel, n_slabs=2),
        grid=(n_n, t // tm),
        in_specs=[pl.BlockSpec((tm, ka), lambda j, i: (i, 0)),
                  pl.BlockSpec((tm, kb), lambda j, i: (i, 0)),
                  pl.BlockSpec((None, ka, tn), lambda j, i: (layer, 0, j)),
                  pl.BlockSpec((None, kb, tn), lambda j, i: (layer, 0, j)),
                  pl.BlockSpec((tm, tn), lambda j, i: (i, j)),
                  pl.BlockSpec((tm, tn), lambda j, i: (i, n_n + j))],
        out_specs=pl.BlockSpec((tm, tn), lambda j, i: (i, j)),
        out_shape=jax.ShapeDtypeStruct((t, d), BF16),
        scratch_shapes=[pltpu.VMEM((ka, tn), BF16), pltpu.VMEM((kb, tn), BF16)],
        compiler_params=pltpu.CompilerParams(
            dimension_semantics=("parallel", "arbitrary"),
            vmem_limit_bytes=_vmem_limit(blocks, scratch)),
        name="gated_proj",
    )(o_a, o_b, w_a, w_b, gates, gates)


def _tile_indices():
    kk = lax.broadcasted_iota(jnp.int32, (ATTN_BLOCK, ATTN_BLOCK), 0)
    qq = lax.broadcasted_iota(jnp.int32, (ATTN_BLOCK, ATTN_BLOCK), 1)
    return kk, qq


def _key_block(k_ref, j, g):
    rows = pl.ds(pl.multiple_of(j * ATTN_BLOCK, ATTN_BLOCK), ATTN_BLOCK)
    return k_ref[rows, g * HEAD_DIM:(g + 1) * HEAD_DIM]


def _store_alibi_tiles(alibi_ref, slopes2):
    kk, qq = _tile_indices()
    dist = (kk - qq).astype(F32)
    for g, slope2 in enumerate(slopes2):
        alibi_ref[0, g] = slope2 * dist
        alibi_ref[1, g] = jnp.where(kk <= qq, slope2 * dist, NEG)


def _flash_blocks(i, n_units, *, scores, values, alibi, bias, acc_ref):
    units = range(n_units)

    def sweep(rs, ms):
        ms = list(ms)
        blocks = [jnp.where(r == 0, i, r - 1) for r in rs]
        owns = [jnp.where(r == 0, 1, 0) for r in rs]
        ss = [[scores(j, u) for u in units] for j in blocks]
        ps, alphas = [], []
        for n, j in enumerate(blocks):
            ps.append([])
            alphas.append([])
            for u in units:
                t = ss[n][u] + alibi(u, owns[n])
                b = bias(j, u)
                m_new = jnp.maximum(ms[u], jnp.max(t, axis=0, keepdims=True) + b)
                ps[n].append(jnp.exp2(t - (m_new - b)).astype(BF16))
                alphas[n].append(jnp.exp2(ms[u] - m_new))
                ms[u] = m_new
        for u in units:
            acc = acc_ref[u]
            for n, j in enumerate(blocks):
                acc = alphas[n][u] * acc + jnp.dot(values(j, u), ps[n][u],
                                                   preferred_element_type=F32)
            acc_ref[u] = acc
        return tuple(ms)

    for u in units:
        acc_ref[u] = jnp.zeros(acc_ref.shape[1:], acc_ref.dtype)
    ms = tuple(jnp.full((1, ATTN_BLOCK), NEG, F32) for _ in units)

    n_blocks = i + 1
    ms = lax.fori_loop(0, n_blocks // 2, lambda k, c: sweep([2 * k, 2 * k + 1], c), ms)

    @pl.when(lax.rem(n_blocks, 2) == 1)
    def _():
        sweep([i], ms)


def _moba_kernel(slopes_ref, q_ref, k_ref, vt_ref, o_ref, kmean_ref, alibi_ref, bias_ref, acc_ref,
                 *, group):
    n_blocks = kmean_ref.shape[0]
    seq = q_ref.shape[0]
    hg = pl.program_id(1)
    i = pl.program_id(2)

    @pl.when(i == 0)
    def _():
        slopes2 = [slopes_ref[hg * group + g] * LOG2E for g in range(group)]
        for j in range(n_blocks):
            kj = k_ref[j * ATTN_BLOCK:(j + 1) * ATTN_BLOCK, :].astype(F32)
            kmean_ref[j:j + 1, :] = jnp.sum(kj, axis=0, keepdims=True) * (1.0 / ATTN_BLOCK)
        _store_alibi_tiles(alibi_ref, slopes2)

        row = lax.broadcasted_iota(jnp.int32, (n_blocks, seq), 0)
        own = lax.broadcasted_iota(jnp.int32, (n_blocks, seq), 1) // ATTN_BLOCK
        for g in range(group):
            cols = slice(g * HEAD_DIM, (g + 1) * HEAD_DIM)
            terms, rest = [], kmean_ref[:, cols]
            for _ in range(3):
                terms.append(rest.astype(BF16).astype(F32))
                rest = rest - terms[-1]
            terms.append(jnp.zeros((HEAD_DIM - 3 * n_blocks, HEAD_DIM), F32))
            stacked = jnp.concatenate(terms, axis=0).astype(BF16)
            parts = lax.dot_general(q_ref[:, cols], stacked, _NT_DIMS,
                                    preferred_element_type=F32).T
            gate = (parts[:n_blocks] + parts[n_blocks:2 * n_blocks]
                    + parts[2 * n_blocks:3 * n_blocks])
            rank = jnp.zeros(gate.shape, jnp.int32)
            for jp in range(n_blocks):
                g_jp = gate[jp:jp + 1, :]
                beats = ((g_jp > gate) | ((g_jp == gate) & (jp < row))) & (jp < own)
                rank = rank + jnp.where(beats, 1, 0)
            selected = (row < own) & (rank < MOBA_TOPK)
            block_off = slopes2[g] * ((row - own) * ATTN_BLOCK).astype(F32)
            bias = jnp.where(selected | (row == own), block_off, NEG)
            for qi in range(n_blocks):
                bias_ref[g, qi] = bias[:, qi * ATTN_BLOCK:(qi + 1) * ATTN_BLOCK]

    q_rows = pl.ds(pl.multiple_of(i * ATTN_BLOCK, ATTN_BLOCK), ATTN_BLOCK)
    qs = [q_ref[q_rows, g * HEAD_DIM:(g + 1) * HEAD_DIM] for g in range(group)]

    _flash_blocks(
        i, group,
        scores=lambda j, g: lax.dot_general(_key_block(k_ref, j, g), qs[g], _NT_DIMS,
                                            preferred_element_type=F32),
        values=lambda j, g: vt_ref[j, g],
        alibi=lambda g, own: alibi_ref[own, g],
        bias=lambda j, g: bias_ref[g, i, pl.ds(j, 1), :],
        acc_ref=acc_ref)
    for g in range(group):
        o = acc_ref[g, :HEAD_DIM, :] * (1.0 / acc_ref[g, HEAD_DIM:HEAD_DIM + 1, :])
        o_ref[:, g * HEAD_DIM:(g + 1) * HEAD_DIM] = o.T.astype(o_ref.dtype)


def _moba_attention(qk, vt, slopes, *, batch, seq, heads, q_col, k_col):
    nb = seq // ATTN_BLOCK
    t = batch * seq
    group = math.gcd(heads, MOBA_HEAD_GROUP)
    gw = group * HEAD_DIM
    v_rows = HEAD_DIM + SUM_ROWS
    assert q_col % group == 0 and k_col % group == 0 and vt.shape[1:] == (heads, v_rows, ATTN_BLOCK)
    blocks = (_nbytes((ATTN_BLOCK, gw), BF16) + 2 * _nbytes((seq, gw), BF16)
              + _nbytes((nb, group, v_rows, ATTN_BLOCK), BF16))
    scratch_types = [((nb, gw), F32), ((2, group, ATTN_BLOCK, ATTN_BLOCK), F32),
                     ((group, nb, nb, ATTN_BLOCK), F32), ((group, v_rows, ATTN_BLOCK), F32)]
    scratch = [pltpu.VMEM(shape, dtype) for shape, dtype in scratch_types]
    scratch_bytes = sum(_nbytes(shape, dtype) for shape, dtype in scratch_types)
    return pl.pallas_call(
        functools.partial(_moba_kernel, group=group),
        grid=(batch, heads // group, nb),
        in_specs=[pl.BlockSpec(memory_space=pltpu.SMEM),
                  pl.BlockSpec((seq, gw), lambda b, h, i: (b, q_col // group + h)),
                  pl.BlockSpec((seq, gw), lambda b, h, i: (b, k_col // group + h)),
                  pl.BlockSpec((nb, group, v_rows, ATTN_BLOCK), lambda b, h, i: (b, h, 0, 0))],
        out_specs=pl.BlockSpec((ATTN_BLOCK, gw), lambda b, h, i: (b * nb + i, h)),
        out_shape=jax.ShapeDtypeStruct((t, heads * HEAD_DIM), BF16),
        scratch_shapes=scratch,
        compiler_params=pltpu.CompilerParams(
            dimension_semantics=("parallel", "parallel", "arbitrary"),
            vmem_limit_bytes=_vmem_limit(blocks, scratch_bytes)),
        name="moba_attention",
    )(slopes, qk, qk, vt)


def _diff_kernel(slopes_ref, lamv_ref, g_ref, q_ref, k_ref, vt_ref, o_ref, alibi_ref, acc_ref,
                 *, lam_init, group):
    hg = pl.program_id(1)
    i = pl.program_id(2)
    slopes2 = [slopes_ref[hg * group + g] * LOG2E for g in range(group)]

    @pl.when(i == 0)
    def _():
        _store_alibi_tiles(alibi_ref, slopes2)

    qs = [q_ref[:, c * HEAD_DIM:(c + 1) * HEAD_DIM] for c in range(2 * group)]

    _flash_blocks(
        i, 2 * group,
        scores=lambda j, c: lax.dot_general(_key_block(k_ref, j, c), qs[c], _NT_DIMS,
                                            preferred_element_type=F32),
        values=lambda j, c: vt_ref[j, c // 2],
        alibi=lambda c, own: alibi_ref[own, c // 2],
        bias=lambda j, c: slopes2[c // 2] * ((j - i) * ATTN_BLOCK).astype(F32),
        acc_ref=acc_ref)

    lv = lamv_ref[...]
    lam = (jnp.exp(jnp.sum(lv[0:1] * lv[1:2], axis=-1, keepdims=True))
           - jnp.exp(jnp.sum(lv[2:3] * lv[3:4], axis=-1, keepdims=True)) + lam_init)

    def normalised(c):
        return acc_ref[c, :DIFF_V_DIM, :] * (1.0 / acc_ref[c, DIFF_V_DIM:DIFF_V_DIM + 1, :])

    gain = g_ref[...] * (1.0 - lam_init)
    for g in range(group):
        o = normalised(2 * g) - lam * normalised(2 * g + 1)
        o = o * lax.rsqrt(jnp.mean(o * o, axis=0, keepdims=True) + EPS)
        o_ref[:, g * DIFF_V_DIM:(g + 1) * DIFF_V_DIM] = (o.T * gain).astype(o_ref.dtype)


def _diff_attention(qk, vt, slopes, lam_vecs, g_subln, *, batch, seq, heads, q_col, k_col,
                    lam_init):
    nb = seq // ATTN_BLOCK
    t = batch * seq
    group = math.gcd(heads, DIFF_HEAD_GROUP)
    gw = group * DIFF_V_DIM
    v_rows = DIFF_V_DIM + SUM_ROWS
    assert q_col % group == 0 and k_col % group == 0 and vt.shape[1:] == (heads, v_rows, ATTN_BLOCK)
    blocks = (_nbytes((ATTN_BLOCK, gw), BF16) + _nbytes((seq, gw), BF16)
              + _nbytes((nb, group, v_rows, ATTN_BLOCK), BF16) + _nbytes((ATTN_BLOCK, gw), BF16))
    scratch_types = [((2, group, ATTN_BLOCK, ATTN_BLOCK), F32),
                     ((2 * group, v_rows, ATTN_BLOCK), F32)]
    scratch_bytes = sum(_nbytes(shape, dtype) for shape, dtype in scratch_types)
    return pl.pallas_call(
        functools.partial(_diff_kernel, lam_init=lam_init, group=group),
        grid=(batch, heads // group, nb),
        in_specs=[pl.BlockSpec(memory_space=pltpu.SMEM),
                  pl.BlockSpec((4, HEAD_DIM), lambda b, h, i: (0, 0)),
                  pl.BlockSpec((1, DIFF_V_DIM), lambda b, h, i: (0, 0)),
                  pl.BlockSpec((ATTN_BLOCK, gw), lambda b, h, i: (b * nb + i, q_col // group + h)),
                  pl.BlockSpec((seq, gw), lambda b, h, i: (b, k_col // group + h)),
                  pl.BlockSpec((nb, group, v_rows, ATTN_BLOCK), lambda b, h, i: (b, h, 0, 0))],
        out_specs=pl.BlockSpec((ATTN_BLOCK, gw), lambda b, h, i: (b * nb + i, h)),
        out_shape=jax.ShapeDtypeStruct((t, heads * DIFF_V_DIM), BF16),
        scratch_shapes=[pltpu.VMEM(shape, dtype) for shape, dtype in scratch_types],
        compiler_params=pltpu.CompilerParams(
            dimension_semantics=("parallel", "parallel", "arbitrary"),
            vmem_limit_bytes=_vmem_limit(blocks, scratch_bytes)),
        name="diff_attention",
    )(slopes, lam_vecs, g_subln.reshape(1, DIFF_V_DIM), qk, qk, vt)


def _alibi_slopes(n_heads):
    return 2.0 ** (-8.0 * jnp.arange(1, n_heads + 1, dtype=F32) / n_heads)


def _tile(n, want):
    if n <= want:
        return n
    t = want
    while n % t:
        t -= ATTN_BLOCK
    assert t > 0
    return t


def kernel(x, w_in, w_proj_a, w_proj_b, w_out, w_up, w_down, g_pre_mix, g_post_mix, g_pre_mlp,
           g_post_mlp, g_subln, lam_q1, lam_k1, lam_q2, lam_k2):
    batch, seq, d = x.shape
    depth = w_in.shape[0]
    t = batch * seq
    mw = d // 2
    heads_a = mw // HEAD_DIM
    heads_b = mw // DIFF_V_DIM
    assert seq % ATTN_BLOCK == 0 and mw % DIFF_V_DIM == 0
    assert w_in.shape[2] == 6 * mw + 2 * d

    slopes_a = _alibi_slopes(heads_a)
    slopes_b = _alibi_slopes(heads_b)
    tm = _tile(t, 1024)
    tn = _tile(mw, 1024)
    tn_gated = _tile(mw, 512)
    wt = mw // tn
    qk_tile = lambda j: j + jnp.where(j >= 2 * wt, wt, 0)
    gate_tile = lambda j: 6 * wt + j
    plain_tile = lambda j: j

    q_scale = HEAD_DIM ** -0.5 * LOG2E

    def scale_queries(r, tile):
        return r * jnp.where((tile // wt) % 2 == 0, q_scale, 1.0)

    xf = x.reshape(t, d)
    h = _rmsnorm_cast(xf, g_pre_mix[0])
    for l in range(depth):
        qk = _proj(h, w_in, l, qk_tile, 4 * wt, out_dtype=BF16, tm=tm, tn=tn,
                   epilogue=scale_queries, name="qk_proj")
        vt_a = _proj(h, w_in, l, lambda j: 2 * wt + j, wt, out_dtype=BF16, tm=tm, tn=tn,
                     value_rows=HEAD_DIM, name="moba_value_proj")
        vt_b = _proj(h, w_in, l, lambda j: 5 * wt + j, wt, out_dtype=BF16, tm=tm, tn=tn,
                     value_rows=DIFF_V_DIM, name="diff_value_proj")
        gates = _proj(h, w_in, l, gate_tile, 2 * d // tn, out_dtype=F32, tm=tm, tn=tn,
                      name="gate_proj")

        o_a = _moba_attention(qk, vt_a, slopes_a, batch=batch, seq=seq, heads=heads_a,
                              q_col=0, k_col=heads_a)
        lam_init = 0.8 - 0.6 * math.exp(-0.3 * l)
        lam_vecs = jnp.stack([lam_q1[l], lam_k1[l], lam_q2[l], lam_k2[l]]).astype(F32)
        o_b = _diff_attention(qk, vt_b, slopes_b, lam_vecs, g_subln[l], batch=batch, seq=seq,
                              heads=heads_b, q_col=2 * heads_b, k_col=3 * heads_b,
                              lam_init=lam_init)

        y = _gated_proj(o_a, o_b, w_proj_a, w_proj_b, l, gates, tm=tm, tn=tn_gated)
        mix = _proj(y, w_out, l, plain_tile, d // tn, out_dtype=F32, tm=tm, tn=tn, name="out_proj")
        xf, h = _residual(xf, mix, g_post_mix[l], g_pre_mlp[l])

        u, w_down_bf = _proj(h, w_up, l, plain_tile, 4 * d // tn, out_dtype=BF16, tm=tm, tn=tn,
                             epilogue=_relu_squared, also_cast=w_down, name="mlp_up")
        m = _matmul_ksplit(u, w_down_bf, out_dtype=F32, tm=tm, tn=_tile(d, 1024),
                           tk=_tile(4 * d, 4096), name="mlp_down")
        g_next = g_pre_mix[l + 1] if l + 1 < depth else None
        xf, h = _residual(xf, m, g_post_mlp[l], g_next)

    return xf.reshape(batch, seq, d)
```

```python
import functools
import math

import jax
import jax.numpy as jnp
from jax import lax
from jax.experimental import pallas as pl
from jax.experimental.pallas import tpu as pltpu

F32 = jnp.float32
BF16 = jnp.bfloat16

V7X_VMEM_BYTES = 64 * 2**20
V7X_VMEM_COMPILER_RESERVE = 12 * 2**20

HEAD_DIM = 128
DIFF_V_DIM = 256
ATTN_BLOCK = 256
MOBA_TOPK = 3
EPS = 1e-6
NEG = -1e30
LOG2E = math.log2(math.e)

MOBA_HEAD_GROUP = 8
DIFF_HEAD_GROUP = 4
SUM_ROWS = 16

_NT_DIMS = (((1,), (1,)), ((), ()))


def _vmem_limit(block_bytes, scratch_bytes=0):
    want = 2 * block_bytes + scratch_bytes + V7X_VMEM_COMPILER_RESERVE
    return int(min(want, V7X_VMEM_BYTES - 2 * 2**20))


def _nbytes(shape, dtype):
    return math.prod(shape) * jnp.dtype(dtype).itemsize


def _rms(x):
    return x * lax.rsqrt(jnp.mean(x * x, axis=-1, keepdims=True) + EPS)


def _rmsnorm_cast_kernel(x_ref, g_ref, h_ref):
    h_ref[...] = (_rms(x_ref[...]) * g_ref[...]).astype(h_ref.dtype)


def _rmsnorm_cast(x, g, *, rows=256):
    t, d = x.shape
    assert t % rows == 0
    blocks = _nbytes((rows, d), F32) + _nbytes((rows, d), BF16)
    return pl.pallas_call(
        _rmsnorm_cast_kernel,
        grid=(t // rows,),
        in_specs=[pl.BlockSpec((rows, d), lambda i: (i, 0)),
                  pl.BlockSpec((1, d), lambda i: (0, 0))],
        out_specs=pl.BlockSpec((rows, d), lambda i: (i, 0)),
        out_shape=jax.ShapeDtypeStruct((t, d), BF16),
        compiler_params=pltpu.CompilerParams(
            dimension_semantics=("parallel",), vmem_limit_bytes=_vmem_limit(blocks)),
        name="rmsnorm_cast",
    )(x, g.reshape(1, d))


def _residual_kernel(x_ref, m_ref, gpost_ref, gnext_ref, xo_ref, h_ref):
    xn = x_ref[...] + _rms(m_ref[...]) * gpost_ref[...]
    xo_ref[...] = xn
    h_ref[...] = (_rms(xn) * gnext_ref[...]).astype(h_ref.dtype)


def _residual_last_kernel(x_ref, m_ref, gpost_ref, xo_ref):
    xo_ref[...] = x_ref[...] + _rms(m_ref[...]) * gpost_ref[...]


def _residual(x, m, g_post, g_next, *, rows=256):
    t, d = x.shape
    assert t % rows == 0
    row_spec = pl.BlockSpec((rows, d), lambda i: (i, 0))
    gain_spec = pl.BlockSpec((1, d), lambda i: (0, 0))
    blocks = 3 * _nbytes((rows, d), F32) + _nbytes((rows, d), BF16)
    params = pltpu.CompilerParams(
        dimension_semantics=("parallel",), vmem_limit_bytes=_vmem_limit(blocks))
    if g_next is None:
        return pl.pallas_call(
            _residual_last_kernel,
            grid=(t // rows,),
            in_specs=[row_spec, row_spec, gain_spec],
            out_specs=row_spec,
            out_shape=jax.ShapeDtypeStruct((t, d), F32),
            compiler_params=params,
            name="residual_last",
        )(x, m, g_post.reshape(1, d)), None
    return pl.pallas_call(
        _residual_kernel,
        grid=(t // rows,),
        in_specs=[row_spec, row_spec, gain_spec, gain_spec],
        out_specs=[row_spec, row_spec],
        out_shape=[jax.ShapeDtypeStruct((t, d), F32), jax.ShapeDtypeStruct((t, d), BF16)],
        compiler_params=params,
        name="residual_norm",
    )(x, m, g_post.reshape(1, d), g_next.reshape(1, d))


def _identity(x, tile):
    return x


def _relu_squared(x, tile):
    r = jnp.maximum(x, 0.0)
    return r * r


def _proj_kernel(x_ref, w_hbm, o_ref, wstage_ref, wbf_ref, sem, *, layer, col_tile, n_tiles,
                 epilogue, key_blocked_t):
    j = pl.program_id(0)
    k, tn = wstage_ref.shape

    def weight_copy(tile):
        col = pl.multiple_of(col_tile(tile) * tn, tn)
        return pltpu.make_async_copy(w_hbm.at[layer, :, pl.ds(col, tn)], wstage_ref, sem)

    def product(round_weights):
        slabs = [slice(c * tn, (c + 1) * tn) for c in range(k // tn)] if round_weights else [
            slice(None)]
        acc = None
        for rows in slabs:
            if key_blocked_t:
                if round_weights:
                    wbf_ref[:, rows] = wstage_ref[rows, :].T.astype(BF16)
                part = lax.dot_general(wbf_ref[:, rows], x_ref[:, rows], _NT_DIMS,
                                       preferred_element_type=F32)
            else:
                if round_weights:
                    wbf_ref[rows, :] = wstage_ref[rows, :].astype(BF16)
                part = jnp.dot(x_ref[:, rows], wbf_ref[rows, :], preferred_element_type=F32)
            acc = part if acc is None else acc + part
        return acc

    def store(r):
        r = epilogue(r, j)
        if key_blocked_t:
            n_chunks, n_heads, rows, _ = o_ref.shape
            rows -= SUM_ROWS
            ones = jnp.ones((SUM_ROWS, ATTN_BLOCK), o_ref.dtype)
            for c in range(n_chunks):
                for hh in range(n_heads):
                    o_ref[c, hh, :rows, :] = r[hh * rows:(hh + 1) * rows,
                                               c * ATTN_BLOCK:(c + 1) * ATTN_BLOCK].astype(o_ref.dtype)
                    o_ref[c, hh, rows:, :] = ones
        else:
            o_ref[...] = r.astype(o_ref.dtype)

    first_token_step = pl.program_id(1) == 0

    @pl.when(first_token_step)
    def _():
        @pl.when(j == 0)
        def _():
            weight_copy(j).start()

        weight_copy(j).wait()
        r = product(round_weights=True)

        @pl.when(j + 1 < n_tiles)
        def _():
            weight_copy(j + 1).start()

        store(r)

    @pl.when(jnp.logical_not(first_token_step))
    def _():
        store(product(round_weights=False))


def _proj_and_cast_kernel(x_ref, w_hbm, side_ref, o_ref, side_bf_ref, *scratch, **static):
    side_bf_ref[...] = side_ref[...].astype(side_bf_ref.dtype)
    _proj_kernel(x_ref, w_hbm, o_ref, *scratch, **static)


def _proj(x, w, layer, col_tile, n_tiles, *, out_dtype, tm, tn, epilogue=_identity,
          value_rows=None, also_cast=None, name):
    m, k = x.shape
    assert m % tm == 0 and w.shape[1] == k and w.shape[2] % tn == 0 and k % tn == 0
    n_m = m // tm
    key_blocked_t = value_rows is not None
    if key_blocked_t:
        assert tm % ATTN_BLOCK == 0 and tn % value_rows == 0
        hpt = tn // value_rows
        block = (tm // ATTN_BLOCK, hpt, value_rows + SUM_ROWS, ATTN_BLOCK)
        out_shape = jax.ShapeDtypeStruct((m // ATTN_BLOCK, n_tiles * hpt) + block[2:], out_dtype)
        out_spec = pl.BlockSpec(block, lambda j, i: (i, j, 0, 0))
    else:
        block = (tm, tn)
        out_shape = jax.ShapeDtypeStruct((m, n_tiles * tn), out_dtype)
        out_spec = pl.BlockSpec(block, lambda j, i: (i, j))
    blocks = _nbytes((tm, k), BF16) + _nbytes(block, out_dtype)
    scratch_bytes = _nbytes((k, tn), F32) + _nbytes((k, tn), BF16)
    body = _proj_kernel
    operands = [x, w]
    in_specs = [pl.BlockSpec((tm, k), lambda j, i: (i, 0)), pl.BlockSpec(memory_space=pl.ANY)]
    if also_cast is not None:
        _, side_k, side_n = also_cast.shape
        assert side_k % (n_tiles * n_m) == 0
        slab = (side_k // (n_tiles * n_m), side_n)
        body = _proj_and_cast_kernel
        operands.append(also_cast)
        in_specs.append(pl.BlockSpec((None,) + slab, lambda j, i: (layer, j * n_m + i, 0)))
        out_spec = [out_spec, pl.BlockSpec(slab, lambda j, i: (j * n_m + i, 0))]
        out_shape = [out_shape, jax.ShapeDtypeStruct((side_k, side_n), BF16)]
        blocks += _nbytes(slab, F32) + _nbytes(slab, BF16)
    return pl.pallas_call(
        functools.partial(body, layer=layer, col_tile=col_tile, n_tiles=n_tiles,
                          epilogue=epilogue, key_blocked_t=key_blocked_t),
        grid=(n_tiles, n_m),
        in_specs=in_specs,
        out_specs=out_spec,
        out_shape=out_shape,
        scratch_shapes=[pltpu.VMEM((k, tn), F32),
                        pltpu.VMEM((tn, k) if key_blocked_t else (k, tn), BF16),
                        pltpu.SemaphoreType.DMA(())],
        compiler_params=pltpu.CompilerParams(
            dimension_semantics=("arbitrary", "arbitrary"),
            vmem_limit_bytes=_vmem_limit(blocks, scratch_bytes)),
        name=name,
    )(*operands)


def _matmul_ksplit_kernel(x_ref, w_ref, o_ref, acc_ref, *, n_k):
    k = pl.program_id(2)

    @pl.when(k == 0)
    def _():
        acc_ref[...] = jnp.zeros_like(acc_ref)

    acc_ref[...] += jnp.dot(x_ref[...], w_ref[...], preferred_element_type=F32)

    @pl.when(k == n_k - 1)
    def _():
        o_ref[...] = acc_ref[...].astype(o_ref.dtype)


def _matmul_ksplit(x, w, *, out_dtype, tm, tn, tk, name):
    m, k = x.shape
    _, n = w.shape
    assert m % tm == 0 and n % tn == 0 and k % tk == 0
    n_k = k // tk
    blocks = _nbytes((tm, tk), BF16) + _nbytes((tk, tn), BF16) + _nbytes((tm, tn), out_dtype)
    return pl.pallas_call(
        functools.partial(_matmul_ksplit_kernel, n_k=n_k),
        grid=(m // tm, n // tn, n_k),
        in_specs=[pl.BlockSpec((tm, tk), lambda i, j, kk: (i, kk)),
                  pl.BlockSpec((tk, tn), lambda i, j, kk: (kk, j))],
        out_specs=pl.BlockSpec((tm, tn), lambda i, j, kk: (i, j)),
        out_shape=jax.ShapeDtypeStruct((m, n), out_dtype),
        scratch_shapes=[pltpu.VMEM((tm, tn), F32)],
        compiler_params=pltpu.CompilerParams(
            dimension_semantics=("parallel", "parallel", "arbitrary"),
            vmem_limit_bytes=_vmem_limit(blocks, _nbytes((tm, tn), F32))),
        name=name,
    )(x, w)


def _sigmoid(x):
    return 0.5 * jnp.tanh(0.5 * x) + 0.5


def _gated_proj_kernel(oa_ref, ob_ref, wa_ref, wb_ref, ga_ref, gb_ref, y_ref, wabf_ref, wbbf_ref,
                       *, n_slabs):
    def product(x_ref, w_ref, wbf_ref, round_weights):
        k = w_ref.shape[0]
        slabs = ([slice(c * k // n_slabs, (c + 1) * k // n_slabs) for c in range(n_slabs)]
                 if round_weights else [slice(None)])
        acc = None
        for rows in slabs:
            if round_weights:
                wbf_ref[rows, :] = w_ref[rows, :].astype(BF16)
            part = jnp.dot(x_ref[:, rows], wbf_ref[rows, :], preferred_element_type=F32)
            acc = part if acc is None else acc + part
        return acc

    def merge(round_weights):
        pa = product(oa_ref, wa_ref, wabf_ref, round_weights)
        pb = product(ob_ref, wb_ref, wbbf_ref, round_weights)
        y = _sigmoid(ga_ref[...]) * pa + _sigmoid(gb_ref[...]) * pb
        y_ref[...] = y.astype(y_ref.dtype)

    first_token_step = pl.program_id(1) == 0
    pl.when(first_token_step)(functools.partial(merge, True))
    pl.when(jnp.logical_not(first_token_step))(functools.partial(merge, False))


def _gated_proj(o_a, o_b, w_a, w_b, layer, gates, *, tm, tn):
    t, ka = o_a.shape
    _, kb = o_b.shape
    d = w_a.shape[2]
    assert t % tm == 0 and d % tn == 0
    n_n = d // tn
    blocks = (_nbytes((tm, ka), BF16) + _nbytes((tm, kb), BF16) + _nbytes((ka, tn), F32)
              + _nbytes((kb, tn), F32) + 2 * _nbytes((tm, tn), F32) + _nbytes((tm, tn), BF16))
    scratch = _nbytes((ka, tn), BF16) + _nbytes((kb, tn), BF16)
    return pl.pallas_call(
        functools.partial(_gated_proj_kernel, n_slabs=2),
        grid=(n_n, t // tm),
        in_specs=[pl.BlockSpec((tm, ka), lambda j, i: (i, 0)),
                  pl.BlockSpec((tm, kb), lambda j, i: (i, 0)),
                  pl.BlockSpec((None, ka, tn), lambda j, i: (layer, 0, j)),
                  pl.BlockSpec((None, kb, tn), lambda j, i: (layer, 0, j)),
                  pl.BlockSpec((tm, tn), lambda j, i: (i, j)),
                  pl.BlockSpec((tm, tn), lambda j, i: (i, n_n + j))],
        out_specs=pl.BlockSpec((tm, tn), lambda j, i: (i, j)),
        out_shape=jax.ShapeDtypeStruct((t, d), BF16),
        scratch_shapes=[pltpu.VMEM((ka, tn), BF16), pltpu.VMEM((kb, tn), BF16)],
        compiler_params=pltpu.CompilerParams(
            dimension_semantics=("parallel", "arbitrary"),
            vmem_limit_bytes=_vmem_limit(blocks, scratch)),
        name="gated_proj",
    )(o_a, o_b, w_a, w_b, gates, gates)


def _tile_indices():
    kk = lax.broadcasted_iota(jnp.int32, (ATTN_BLOCK, ATTN_BLOCK), 0)
    qq = lax.broadcasted_iota(jnp.int32, (ATTN_BLOCK, ATTN_BLOCK), 1)
    return kk, qq


def _key_block(k_ref, j, g):
    rows = pl.ds(pl.multiple_of(j * ATTN_BLOCK, ATTN_BLOCK), ATTN_BLOCK)
    return k_ref[rows, g * HEAD_DIM:(g + 1) * HEAD_DIM]


def _store_alibi_tiles(alibi_ref, slopes2):
    kk, qq = _tile_indices()
    dist = (kk - qq).astype(F32)
    for g, slope2 in enumerate(slopes2):
        alibi_ref[0, g] = slope2 * dist
        alibi_ref[1, g] = jnp.where(kk <= qq, slope2 * dist, NEG)


def _flash_blocks(i, n_units, *, scores, values, alibi, bias, acc_ref):
    units = range(n_units)

    def sweep(rs, ms):
        ms = list(ms)
        blocks = [jnp.where(r == 0, i, r - 1) for r in rs]
        owns = [jnp.where(r == 0, 1, 0) for r in rs]
        ss = [[scores(j, u) for u in units] for j in blocks]
        ps, alphas = [], []
        for n, j in enumerate(blocks):
            ps.append([])
            alphas.append([])
            for u in units:
                t = ss[n][u] + alibi(u, owns[n])
                b = bias(j, u)
                m_new = jnp.maximum(ms[u], jnp.max(t, axis=0, keepdims=True) + b)
                ps[n].append(jnp.exp2(t - (m_new - b)).astype(BF16))
                alphas[n].append(jnp.exp2(ms[u] - m_new))
                ms[u] = m_new
        for u in units:
            acc = acc_ref[u]
            for n, j in enumerate(blocks):
                acc = alphas[n][u] * acc + jnp.dot(values(j, u), ps[n][u],
                                                   preferred_element_type=F32)
            acc_ref[u] = acc
        return tuple(ms)

    for u in units:
        acc_ref[u] = jnp.zeros(acc_ref.shape[1:], acc_ref.dtype)
    ms = tuple(jnp.full((1, ATTN_BLOCK), NEG, F32) for _ in units)

    n_blocks = i + 1
    ms = lax.fori_loop(0, n_blocks // 4,
                       lambda k, c: sweep([4 * k, 4 * k + 1, 4 * k + 2, 4 * k + 3], c), ms)
    done = (n_blocks // 4) * 4
    ms = lax.cond(n_blocks - done >= 2, lambda c: sweep([done, done + 1], c), lambda c: c, ms)

    @pl.when(lax.rem(n_blocks, 2) == 1)
    def _():
        sweep([i], ms)


def _moba_kernel(slopes_ref, q_ref, k_ref, vt_ref, o_ref, kmean_ref, alibi_ref, bias_ref, acc_ref,
                 *, group):
    n_blocks = kmean_ref.shape[0]
    seq = q_ref.shape[0]
    hg = pl.program_id(1)
    i = pl.program_id(2)

    @pl.when(i == 0)
    def _():
        slopes2 = [slopes_ref[hg * group + g] * LOG2E for g in range(group)]
        for j in range(n_blocks):
            kj = k_ref[j * ATTN_BLOCK:(j + 1) * ATTN_BLOCK, :].astype(F32)
            kmean_ref[j:j + 1, :] = jnp.sum(kj, axis=0, keepdims=True) * (1.0 / ATTN_BLOCK)
        _store_alibi_tiles(alibi_ref, slopes2)

        row = lax.broadcasted_iota(jnp.int32, (n_blocks, seq), 0)
        own = lax.broadcasted_iota(jnp.int32, (n_blocks, seq), 1) // ATTN_BLOCK
        for g in range(group):
            cols = slice(g * HEAD_DIM, (g + 1) * HEAD_DIM)
            terms, rest = [], kmean_ref[:, cols]
            for _ in range(3):
                terms.append(rest.astype(BF16).astype(F32))
                rest = rest - terms[-1]
            terms.append(jnp.zeros((HEAD_DIM - 3 * n_blocks, HEAD_DIM), F32))
            stacked = jnp.concatenate(terms, axis=0).astype(BF16)
            parts = lax.dot_general(q_ref[:, cols], stacked, _NT_DIMS,
                                    preferred_element_type=F32).T
            gate = (parts[:n_blocks] + parts[n_blocks:2 * n_blocks]
                    + parts[2 * n_blocks:3 * n_blocks])
            rank = jnp.zeros(gate.shape, jnp.int32)
            for jp in range(n_blocks):
                g_jp = gate[jp:jp + 1, :]
                beats = ((g_jp > gate) | ((g_jp == gate) & (jp < row))) & (jp < own)
                rank = rank + jnp.where(beats, 1, 0)
            selected = (row < own) & (rank < MOBA_TOPK)
            block_off = slopes2[g] * ((row - own) * ATTN_BLOCK).astype(F32)
            bias = jnp.where(selected | (row == own), block_off, NEG)
            for qi in range(n_blocks):
                bias_ref[g, qi] = bias[:, qi * ATTN_BLOCK:(qi + 1) * ATTN_BLOCK]

    q_rows = pl.ds(pl.multiple_of(i * ATTN_BLOCK, ATTN_BLOCK), ATTN_BLOCK)
    qs = [q_ref[q_rows, g * HEAD_DIM:(g + 1) * HEAD_DIM] for g in range(group)]

    _flash_blocks(
        i, group,
        scores=lambda j, g: lax.dot_general(_key_block(k_ref, j, g), qs[g], _NT_DIMS,
                                            preferred_element_type=F32),
        values=lambda j, g: vt_ref[j, g],
        alibi=lambda g, own: alibi_ref[own, g],
        bias=lambda j, g: bias_ref[g, i, pl.ds(j, 1), :],
        acc_ref=acc_ref)
    for g in range(group):
        o = acc_ref[g, :HEAD_DIM, :] * (1.0 / acc_ref[g, HEAD_DIM:HEAD_DIM + 1, :])
        o_ref[:, g * HEAD_DIM:(g + 1) * HEAD_DIM] = o.T.astype(o_ref.dtype)


def _moba_attention(qk, vt, slopes, *, batch, seq, heads, q_col, k_col):
    nb = seq // ATTN_BLOCK
    t = batch * seq
    group = math.gcd(heads, MOBA_HEAD_GROUP)
    gw = group * HEAD_DIM
    v_rows = HEAD_DIM + SUM_ROWS
    assert q_col % group == 0 and k_col % group == 0 and vt.shape[1:] == (heads, v_rows, ATTN_BLOCK)
    blocks = (_nbytes((ATTN_BLOCK, gw), BF16) + 2 * _nbytes((seq, gw), BF16)
              + _nbytes((nb, group, v_rows, ATTN_BLOCK), BF16))
    scratch_types = [((nb, gw), F32), ((2, group, ATTN_BLOCK, ATTN_BLOCK), F32),
                     ((group, nb, nb, ATTN_BLOCK), F32), ((group, v_rows, ATTN_BLOCK), F32)]
    scratch = [pltpu.VMEM(shape, dtype) for shape, dtype in scratch_types]
    scratch_bytes = sum(_nbytes(shape, dtype) for shape, dtype in scratch_types)
    return pl.pallas_call(
        functools.partial(_moba_kernel, group=group),
        grid=(batch, heads // group, nb),
        in_specs=[pl.BlockSpec(memory_space=pltpu.SMEM),
                  pl.BlockSpec((seq, gw), lambda b, h, i: (b, q_col // group + h)),
                  pl.BlockSpec((seq, gw), lambda b, h, i: (b, k_col // group + h)),
                  pl.BlockSpec((nb, group, v_rows, ATTN_BLOCK), lambda b, h, i: (b, h, 0, 0))],
        out_specs=pl.BlockSpec((ATTN_BLOCK, gw), lambda b, h, i: (b * nb + i, h)),
        out_shape=jax.ShapeDtypeStruct((t, heads * HEAD_DIM), BF16),
        scratch_shapes=scratch,
        compiler_params=pltpu.CompilerParams(
            dimension_semantics=("parallel", "parallel", "arbitrary"),
            vmem_limit_bytes=_vmem_limit(blocks, scratch_bytes)),
        name="moba_attention",
    )(slopes, qk, qk, vt)


def _diff_kernel(slopes_ref, lamv_ref, g_ref, q_ref, k_ref, vt_ref, o_ref, alibi_ref, acc_ref,
                 *, lam_init, group):
    hg = pl.program_id(1)
    i = pl.program_id(2)
    slopes2 = [slopes_ref[hg * group + g] * LOG2E for g in range(group)]

    @pl.when(i == 0)
    def _():
        _store_alibi_tiles(alibi_ref, slopes2)

    qs = [q_ref[:, c * HEAD_DIM:(c + 1) * HEAD_DIM] for c in range(2 * group)]

    _flash_blocks(
        i, 2 * group,
        scores=lambda j, c: lax.dot_general(_key_block(k_ref, j, c), qs[c], _NT_DIMS,
                                            preferred_element_type=F32),
        values=lambda j, c: vt_ref[j, c // 2],
        alibi=lambda c, own: alibi_ref[own, c // 2],
        bias=lambda j, c: slopes2[c // 2] * ((j - i) * ATTN_BLOCK).astype(F32),
        acc_ref=acc_ref)

    lv = lamv_ref[...]
    lam = (jnp.exp(jnp.sum(lv[0:1] * lv[1:2], axis=-1, keepdims=True))
           - jnp.exp(jnp.sum(lv[2:3] * lv[3:4], axis=-1, keepdims=True)) + lam_init)

    def normalised(c):
        return acc_ref[c, :DIFF_V_DIM, :] * (1.0 / acc_ref[c, DIFF_V_DIM:DIFF_V_DIM + 1, :])

    gain = g_ref[...] * (1.0 - lam_init)
    for g in range(group):
        o = normalised(2 * g) - lam * normalised(2 * g + 1)
        o = o * lax.rsqrt(jnp.mean(o * o, axis=0, keepdims=True) + EPS)
        o_ref[:, g * DIFF_V_DIM:(g + 1) * DIFF_V_DIM] = (o.T * gain).astype(o_ref.dtype)


def _diff_attention(qk, vt, slopes, lam_vecs, g_subln, *, batch, seq, heads, q_col, k_col,
                    lam_init):
    nb = seq // ATTN_BLOCK
    t = batch * seq
    group = math.gcd(heads, DIFF_HEAD_GROUP)
    gw = group * DIFF_V_DIM
    v_rows = DIFF_V_DIM + SUM_ROWS
    assert q_col % group == 0 and k_col % group == 0 and vt.shape[1:] == (heads, v_rows, ATTN_BLOCK)
    blocks = (_nbytes((ATTN_BLOCK, gw), BF16) + _nbytes((seq, gw), BF16)
              + _nbytes((nb, group, v_rows, ATTN_BLOCK), BF16) + _nbytes((ATTN_BLOCK, gw), BF16))
    scratch_types = [((2, group, ATTN_BLOCK, ATTN_BLOCK), F32),
                     ((2 * group, v_rows, ATTN_BLOCK), F32)]
    scratch_bytes = sum(_nbytes(shape, dtype) for shape, dtype in scratch_types)
    return pl.pallas_call(
        functools.partial(_diff_kernel, lam_init=lam_init, group=group),
        grid=(batch, heads // group, nb),
        in_specs=[pl.BlockSpec(memory_space=pltpu.SMEM),
                  pl.BlockSpec((4, HEAD_DIM), lambda b, h, i: (0, 0)),
                  pl.BlockSpec((1, DIFF_V_DIM), lambda b, h, i: (0, 0)),
                  pl.BlockSpec((ATTN_BLOCK, gw), lambda b, h, i: (b * nb + i, q_col // group + h)),
                  pl.BlockSpec((seq, gw), lambda b, h, i: (b, k_col // group + h)),
                  pl.BlockSpec((nb, group, v_rows, ATTN_BLOCK), lambda b, h, i: (b, h, 0, 0))],
        out_specs=pl.BlockSpec((ATTN_BLOCK, gw), lambda b, h, i: (b * nb + i, h)),
        out_shape=jax.ShapeDtypeStruct((t, heads * DIFF_V_DIM), BF16),
        scratch_shapes=[pltpu.VMEM(shape, dtype) for shape, dtype in scratch_types],
        compiler_params=pltpu.CompilerParams(
            dimension_semantics=("parallel", "parallel", "arbitrary"),
            vmem_limit_bytes=_vmem_limit(blocks, scratch_bytes)),
        name="diff_attention",
    )(slopes, lam_vecs, g_subln.reshape(1, DIFF_V_DIM), qk, qk, vt)


def _alibi_slopes(n_heads):
    return 2.0 ** (-8.0 * jnp.arange(1, n_heads + 1, dtype=F32) / n_heads)


def _tile(n, want):
    if n <= want:
        return n
    t = want
    while n % t:
        t -= ATTN_BLOCK
    assert t > 0
    return t


def kernel(x, w_in, w_proj_a, w_proj_b, w_out, w_up, w_down, g_pre_mix, g_post_mix, g_pre_mlp,
           g_post_mlp, g_subln, lam_q1, lam_k1, lam_q2, lam_k2):
    batch, seq, d = x.shape
    depth = w_in.shape[0]
    t = batch * seq
    mw = d // 2
    heads_a = mw // HEAD_DIM
    heads_b = mw // DIFF_V_DIM
    assert seq % ATTN_BLOCK == 0 and mw % DIFF_V_DIM == 0
    assert w_in.shape[2] == 6 * mw + 2 * d

    slopes_a = _alibi_slopes(heads_a)
    slopes_b = _alibi_slopes(heads_b)
    tm = _tile(t, 1024)
    tn = _tile(mw, 1024)
    tn_gated = _tile(mw, 512)
    wt = mw // tn
    qk_tile = lambda j: j + jnp.where(j >= 2 * wt, wt, 0)
    gate_tile = lambda j: 6 * wt + j
    plain_tile = lambda j: j

    q_scale = HEAD_DIM ** -0.5 * LOG2E

    def scale_queries(r, tile):
        return r * jnp.where((tile // wt) % 2 == 0, q_scale, 1.0)

    xf = x.reshape(t, d)
    h = _rmsnorm_cast(xf, g_pre_mix[0])
    for l in range(depth):
        qk = _proj(h, w_in, l, qk_tile, 4 * wt, out_dtype=BF16, tm=tm, tn=tn,
                   epilogue=scale_queries, name="qk_proj")
        vt_a = _proj(h, w_in, l, lambda j: 2 * wt + j, wt, out_dtype=BF16, tm=tm, tn=tn,
                     value_rows=HEAD_DIM, name="moba_value_proj")
        vt_b = _proj(h, w_in, l, lambda j: 5 * wt + j, wt, out_dtype=BF16, tm=tm, tn=tn,
                     value_rows=DIFF_V_DIM, name="diff_value_proj")
        gates = _proj(h, w_in, l, gate_tile, 2 * d // tn, out_dtype=F32, tm=tm, tn=tn,
                      name="gate_proj")

        o_a = _moba_attention(qk, vt_a, slopes_a, batch=batch, seq=seq, heads=heads_a,
                              q_col=0, k_col=heads_a)
        lam_init = 0.8 - 0.6 * math.exp(-0.3 * l)
        lam_vecs = jnp.stack([lam_q1[l], lam_k1[l], lam_q2[l], lam_k2[l]]).astype(F32)
        o_b = _diff_attention(qk, vt_b, slopes_b, lam_vecs, g_subln[l], batch=batch, seq=seq,
                              heads=heads_b, q_col=2 * heads_b, k_col=3 * heads_b,
                              lam_init=lam_init)

        y = _gated_proj(o_a, o_b, w_proj_a, w_proj_b, l, gates, tm=tm, tn=tn_gated)
        mix = _proj(y, w_out, l, plain_tile, d // tn, out_dtype=F32, tm=tm, tn=tn, name="out_proj")
        xf, h = _residual(xf, mix, g_post_mix[l], g_pre_mlp[l])

        u, w_down_bf = _proj(h, w_up, l, plain_tile, 4 * d // tn, out_dtype=BF16, tm=tm, tn=tn,
                             epilogue=_relu_squared, also_cast=w_down, name="mlp_up")
        m = _matmul_ksplit(u, w_down_bf, out_dtype=F32, tm=tm, tn=_tile(d, 1024),
                           tk=_tile(4 * d, 4096), name="mlp_down")
        g_next = g_pre_mix[l + 1] if l + 1 < depth else None
        xf, h = _residual(xf, m, g_post_mlp[l], g_next)

    return xf.reshape(batch, seq, d)
```

```python
import functools
import math

import jax
import jax.numpy as jnp
from jax import lax
from jax.experimental import pallas as pl
from jax.experimental.pallas import tpu as pltpu

F32 = jnp.float32
BF16 = jnp.bfloat16

V7X_VMEM_BYTES = 64 * 2**20
V7X_VMEM_COMPILER_RESERVE = 12 * 2**20

HEAD_DIM = 128
DIFF_V_DIM = 256
ATTN_BLOCK = 256
MOBA_TOPK = 3
EPS = 1e-6
NEG = -1e30
LOG2E = math.log2(math.e)

MOBA_HEAD_GROUP = 8
DIFF_HEAD_GROUP = 4
SUM_ROWS = 16

_NT_DIMS = (((1,), (1,)), ((), ()))


def _vmem_limit(block_bytes, scratch_bytes=0):
    want = 2 * block_bytes + scratch_bytes + V7X_VMEM_COMPILER_RESERVE
    return int(min(want, V7X_VMEM_BYTES - 2 * 2**20))


def _nbytes(shape, dtype):
    return math.prod(shape) * jnp.dtype(dtype).itemsize


def _rms(x):
    return x * lax.rsqrt(jnp.mean(x * x, axis=-1, keepdims=True) + EPS)


def _rmsnorm_cast_kernel(x_ref, g_ref, h_ref):
    h_ref[...] = (_rms(x_ref[...]) * g_ref[...]).astype(h_ref.dtype)


def _rmsnorm_cast(x, g, *, rows=256):
    t, d = x.shape
    assert t % rows == 0
    blocks = _nbytes((rows, d), F32) + _nbytes((rows, d), BF16)
    return pl.pallas_call(
        _rmsnorm_cast_kernel,
        grid=(t // rows,),
        in_specs=[pl.BlockSpec((rows, d), lambda i: (i, 0)),
                  pl.BlockSpec((1, d), lambda i: (0, 0))],
        out_specs=pl.BlockSpec((rows, d), lambda i: (i, 0)),
        out_shape=jax.ShapeDtypeStruct((t, d), BF16),
        compiler_params=pltpu.CompilerParams(
            dimension_semantics=("parallel",), vmem_limit_bytes=_vmem_limit(blocks)),
        name="rmsnorm_cast",
    )(x, g.reshape(1, d))


def _residual_kernel(x_ref, m_ref, gpost_ref, gnext_ref, xo_ref, h_ref):
    xn = x_ref[...] + _rms(m_ref[...]) * gpost_ref[...]
    xo_ref[...] = xn
    h_ref[...] = (_rms(xn) * gnext_ref[...]).astype(h_ref.dtype)


def _residual_last_kernel(x_ref, m_ref, gpost_ref, xo_ref):
    xo_ref[...] = x_ref[...] + _rms(m_ref[...]) * gpost_ref[...]


def _residual(x, m, g_post, g_next, *, rows=256):
    t, d = x.shape
    assert t % rows == 0
    row_spec = pl.BlockSpec((rows, d), lambda i: (i, 0))
    gain_spec = pl.BlockSpec((1, d), lambda i: (0, 0))
    blocks = 3 * _nbytes((rows, d), F32) + _nbytes((rows, d), BF16)
    params = pltpu.CompilerParams(
        dimension_semantics=("parallel",), vmem_limit_bytes=_vmem_limit(blocks))
    if g_next is None:
        return pl.pallas_call(
            _residual_last_kernel,
            grid=(t // rows,),
            in_specs=[row_spec, row_spec, gain_spec],
            out_specs=row_spec,
            out_shape=jax.ShapeDtypeStruct((t, d), F32),
            compiler_params=params,
            name="residual_last",
        )(x, m, g_post.reshape(1, d)), None
    return pl.pallas_call(
        _residual_kernel,
        grid=(t // rows,),
        in_specs=[row_spec, row_spec, gain_spec, gain_spec],
        out_specs=[row_spec, row_spec],
        out_shape=[jax.ShapeDtypeStruct((t, d), F32), jax.ShapeDtypeStruct((t, d), BF16)],
        compiler_params=params,
        name="residual_norm",
    )(x, m, g_post.reshape(1, d), g_next.reshape(1, d))


def _identity(x, tile):
    return x


def _relu_squared(x, tile):
    r = jnp.maximum(x, 0.0)
    return r * r


def _proj_kernel(x_ref, w_hbm, o_ref, wstage_ref, wbf_ref, sem, *, layer, col_tile, n_tiles,
                 epilogue, key_blocked_t):
    j = pl.program_id(0)
    k, tn = wstage_ref.shape

    def weight_copy(tile):
        col = pl.multiple_of(col_tile(tile) * tn, tn)
        return pltpu.make_async_copy(w_hbm.at[layer, :, pl.ds(col, tn)], wstage_ref, sem)

    def product(round_weights):
        slabs = [slice(c * tn, (c + 1) * tn) for c in range(k // tn)] if round_weights else [
            slice(None)]
        acc = None
        for rows in slabs:
            if key_blocked_t:
                if round_weights:
                    wbf_ref[:, rows] = wstage_ref[rows, :].T.astype(BF16)
                part = lax.dot_general(wbf_ref[:, rows], x_ref[:, rows], _NT_DIMS,
                                       preferred_element_type=F32)
            else:
                if round_weights:
                    wbf_ref[rows, :] = wstage_ref[rows, :].astype(BF16)
                part = jnp.dot(x_ref[:, rows], wbf_ref[rows, :], preferred_element_type=F32)
            acc = part if acc is None else acc + part
        return acc

    def store(r):
        r = epilogue(r, j)
        if key_blocked_t:
            n_chunks, n_heads, rows, _ = o_ref.shape
            rows -= SUM_ROWS
            ones = jnp.ones((SUM_ROWS, ATTN_BLOCK), o_ref.dtype)
            for c in range(n_chunks):
                for hh in range(n_heads):
                    o_ref[c, hh, :rows, :] = r[hh * rows:(hh + 1) * rows,
                                               c * ATTN_BLOCK:(c + 1) * ATTN_BLOCK].astype(o_ref.dtype)
                    o_ref[c, hh, rows:, :] = ones
        else:
            o_ref[...] = r.astype(o_ref.dtype)

    first_token_step = pl.program_id(1) == 0

    @pl.when(first_token_step)
    def _():
        @pl.when(j == 0)
        def _():
            weight_copy(j).start()

        weight_copy(j).wait()
        r = product(round_weights=True)

        @pl.when(j + 1 < n_tiles)
        def _():
            weight_copy(j + 1).start()

        store(r)

    @pl.when(jnp.logical_not(first_token_step))
    def _():
        store(product(round_weights=False))


def _proj_and_cast_kernel(x_ref, w_hbm, side_ref, o_ref, side_bf_ref, *scratch, **static):
    side_bf_ref[...] = side_ref[...].astype(side_bf_ref.dtype)
    _proj_kernel(x_ref, w_hbm, o_ref, *scratch, **static)


def _proj(x, w, layer, col_tile, n_tiles, *, out_dtype, tm, tn, epilogue=_identity,
          value_rows=None, also_cast=None, name):
    m, k = x.shape
    assert m % tm == 0 and w.shape[1] == k and w.shape[2] % tn == 0 and k % tn == 0
    n_m = m // tm
    key_blocked_t = value_rows is not None
    if key_blocked_t:
        assert tm % ATTN_BLOCK == 0 and tn % value_rows == 0
        hpt = tn // value_rows
        block = (tm // ATTN_BLOCK, hpt, value_rows + SUM_ROWS, ATTN_BLOCK)
        out_shape = jax.ShapeDtypeStruct((m // ATTN_BLOCK, n_tiles * hpt) + block[2:], out_dtype)
        out_spec = pl.BlockSpec(block, lambda j, i: (i, j, 0, 0))
    else:
        block = (tm, tn)
        out_shape = jax.ShapeDtypeStruct((m, n_tiles * tn), out_dtype)
        out_spec = pl.BlockSpec(block, lambda j, i: (i, j))
    blocks = _nbytes((tm, k), BF16) + _nbytes(block, out_dtype)
    scratch_bytes = _nbytes((k, tn), F32) + _nbytes((k, tn), BF16)
    body = _proj_kernel
    operands = [x, w]
    in_specs = [pl.BlockSpec((tm, k), lambda j, i: (i, 0)), pl.BlockSpec(memory_space=pl.ANY)]
    if also_cast is not None:
        _, side_k, side_n = also_cast.shape
        assert side_k % (n_tiles * n_m) == 0
        slab = (side_k // (n_tiles * n_m), side_n)
        body = _proj_and_cast_kernel
        operands.append(also_cast)
        in_specs.append(pl.BlockSpec((None,) + slab, lambda j, i: (layer, j * n_m + i, 0)))
        out_spec = [out_spec, pl.BlockSpec(slab, lambda j, i: (j * n_m + i, 0))]
        out_shape = [out_shape, jax.ShapeDtypeStruct((side_k, side_n), BF16)]
        blocks += _nbytes(slab, F32) + _nbytes(slab, BF16)
    return pl.pallas_call(
        functools.partial(body, layer=layer, col_tile=col_tile, n_tiles=n_tiles,
                          epilogue=epilogue, key_blocked_t=key_blocked_t),
        grid=(n_tiles, n_m),
        in_specs=in_specs,
        out_specs=out_spec,
        out_shape=out_shape,
        scratch_shapes=[pltpu.VMEM((k, tn), F32),
                        pltpu.VMEM((tn, k) if key_blocked_t else (k, tn), BF16),
                        pltpu.SemaphoreType.DMA(())],
        compiler_params=pltpu.CompilerParams(
            dimension_semantics=("arbitrary", "arbitrary"),
            vmem_limit_bytes=_vmem_limit(blocks, scratch_bytes)),
        name=name,
    )(*operands)


def _matmul_ksplit_kernel(x_ref, w_ref, o_ref, acc_ref, *, n_k):
    k = pl.program_id(2)

    @pl.when(k == 0)
    def _():
        acc_ref[...] = jnp.zeros_like(acc_ref)

    acc_ref[...] += jnp.dot(x_ref[...], w_ref[...], preferred_element_type=F32)

    @pl.when(k == n_k - 1)
    def _():
        o_ref[...] = acc_ref[...].astype(o_ref.dtype)


def _matmul_ksplit(x, w, *, out_dtype, tm, tn, tk, name):
    m, k = x.shape
    _, n = w.shape
    assert m % tm == 0 and n % tn == 0 and k % tk == 0
    n_k = k // tk
    blocks = _nbytes((tm, tk), BF16) + _nbytes((tk, tn), BF16) + _nbytes((tm, tn), out_dtype)
    return pl.pallas_call(
        functools.partial(_matmul_ksplit_kernel, n_k=n_k),
        grid=(m // tm, n // tn, n_k),
        in_specs=[pl.BlockSpec((tm, tk), lambda i, j, kk: (i, kk)),
                  pl.BlockSpec((tk, tn), lambda i, j, kk: (kk, j))],
        out_specs=pl.BlockSpec((tm, tn), lambda i, j, kk: (i, j)),
        out_shape=jax.ShapeDtypeStruct((m, n), out_dtype),
        scratch_shapes=[pltpu.VMEM((tm, tn), F32)],
        compiler_params=pltpu.CompilerParams(
            dimension_semantics=("parallel", "parallel", "arbitrary"),
            vmem_limit_bytes=_vmem_limit(blocks, _nbytes((tm, tn), F32))),
        name=name,
    )(x, w)


def _sigmoid(x):
    return 0.5 * jnp.tanh(0.5 * x) + 0.5


def _gated_proj_kernel(oa_ref, ob_ref, wa_hbm, wb_hbm, ga_ref, gb_ref, y_ref, wa_stage, wb_stage,
                       wabf_ref, wbbf_ref, sem, *, layer, n_tiles):
    j = pl.program_id(0)
    tn = wa_stage.shape[1]

    def weight_copies(tile):
        col = pl.multiple_of(tile * tn, tn)
        return [pltpu.make_async_copy(w_hbm.at[layer, :, pl.ds(col, tn)], stage, sem.at[n])
                for n, (w_hbm, stage) in enumerate(((wa_hbm, wa_stage), (wb_hbm, wb_stage)))]

    def product(x_ref, stage_ref, wbf_ref, round_weights):
        k = stage_ref.shape[0]
        slabs = ([slice(c * tn, (c + 1) * tn) for c in range(k // tn)] if round_weights else
                 [slice(None)])
        acc = None
        for rows in slabs:
            if round_weights:
                wbf_ref[rows, :] = stage_ref[rows, :].astype(BF16)
            part = jnp.dot(x_ref[:, rows], wbf_ref[rows, :], preferred_element_type=F32)
            acc = part if acc is None else acc + part
        return acc

    def products(round_weights):
        return (product(oa_ref, wa_stage, wabf_ref, round_weights),
                product(ob_ref, wb_stage, wbbf_ref, round_weights))

    def merge(pa, pb):
        y = _sigmoid(ga_ref[...]) * pa + _sigmoid(gb_ref[...]) * pb
        y_ref[...] = y.astype(y_ref.dtype)

    first_token_step = pl.program_id(1) == 0

    @pl.when(first_token_step)
    def _():
        @pl.when(j == 0)
        def _():
            for copy in weight_copies(j):
                copy.start()

        for copy in weight_copies(j):
            copy.wait()
        pa, pb = products(round_weights=True)

        @pl.when(j + 1 < n_tiles)
        def _():
            for copy in weight_copies(j + 1):
                copy.start()

        merge(pa, pb)

    @pl.when(jnp.logical_not(first_token_step))
    def _():
        merge(*products(round_weights=False))


def _gated_proj(o_a, o_b, w_a, w_b, layer, gates, *, tm, tn):
    t, ka = o_a.shape
    _, kb = o_b.shape
    d = w_a.shape[2]
    assert t % tm == 0 and d % tn == 0 and ka % tn == 0 and kb % tn == 0
    n_n = d // tn
    blocks = (_nbytes((tm, ka), BF16) + _nbytes((tm, kb), BF16) + 2 * _nbytes((tm, tn), F32)
              + _nbytes((tm, tn), BF16))
    scratch_types = [((ka, tn), F32), ((kb, tn), F32), ((ka, tn), BF16), ((kb, tn), BF16)]
    scratch_bytes = sum(_nbytes(shape, dtype) for shape, dtype in scratch_types)
    return pl.pallas_call(
        functools.partial(_gated_proj_kernel, layer=layer, n_tiles=n_n),
        grid=(n_n, t // tm),
        in_specs=[pl.BlockSpec((tm, ka), lambda j, i: (i, 0)),
                  pl.BlockSpec((tm, kb), lambda j, i: (i, 0)),
                  pl.BlockSpec(memory_space=pl.ANY),
                  pl.BlockSpec(memory_space=pl.ANY),
                  pl.BlockSpec((tm, tn), lambda j, i: (i, j)),
                  pl.BlockSpec((tm, tn), lambda j, i: (i, n_n + j))],
        out_specs=pl.BlockSpec((tm, tn), lambda j, i: (i, j)),
        out_shape=jax.ShapeDtypeStruct((t, d), BF16),
        scratch_shapes=[pltpu.VMEM(shape, dtype) for shape, dtype in scratch_types]
        + [pltpu.SemaphoreType.DMA((2,))],
        compiler_params=pltpu.CompilerParams(
            dimension_semantics=("arbitrary", "arbitrary"),
            vmem_limit_bytes=_vmem_limit(blocks, scratch_bytes)),
        name="gated_proj",
    )(o_a, o_b, w_a, w_b, gates, gates)


def _tile_indices():
    kk = lax.broadcasted_iota(jnp.int32, (ATTN_BLOCK, ATTN_BLOCK), 0)
    qq = lax.broadcasted_iota(jnp.int32, (ATTN_BLOCK, ATTN_BLOCK), 1)
    return kk, qq


def _key_block(k_ref, j, g):
    rows = pl.ds(pl.multiple_of(j * ATTN_BLOCK, ATTN_BLOCK), ATTN_BLOCK)
    return k_ref[rows, g * HEAD_DIM:(g + 1) * HEAD_DIM]


def _store_alibi_tiles(alibi_ref, slopes2):
    kk, qq = _tile_indices()
    dist = (kk - qq).astype(F32)
    for g, slope2 in enumerate(slopes2):
        alibi_ref[0, g] = slope2 * dist
        alibi_ref[1, g] = jnp.where(kk <= qq, slope2 * dist, NEG)


def _flash_blocks(i, n_units, *, scores, values, alibi, bias, acc_ref):
    units = range(n_units)

    def sweep(rs, ms):
        ms = list(ms)
        blocks = [jnp.where(r == 0, i, r - 1) for r in rs]
        owns = [jnp.where(r == 0, 1, 0) for r in rs]
        ss = [[scores(j, u) for u in units] for j in blocks]
        ps, alphas = [], []
        for n, j in enumerate(blocks):
            ps.append([])
            alphas.append([])
            for u in units:
                t = ss[n][u] + alibi(u, owns[n])
                b = bias(j, u)
                m_new = jnp.maximum(ms[u], jnp.max(t, axis=0, keepdims=True) + b)
                ps[n].append(jnp.exp2(t - (m_new - b)).astype(BF16))
                alphas[n].append(jnp.exp2(ms[u] - m_new))
                ms[u] = m_new
        for u in units:
            acc = acc_ref[u]
            for n, j in enumerate(blocks):
                acc = alphas[n][u] * acc + jnp.dot(values(j, u), ps[n][u],
                                                   preferred_element_type=F32)
            acc_ref[u] = acc
        return tuple(ms)

    for u in units:
        acc_ref[u] = jnp.zeros(acc_ref.shape[1:], acc_ref.dtype)
    ms = tuple(jnp.full((1, ATTN_BLOCK), NEG, F32) for _ in units)

    n_blocks = i + 1
    ms = lax.fori_loop(0, n_blocks // 4,
                       lambda k, c: sweep([4 * k, 4 * k + 1, 4 * k + 2, 4 * k + 3], c), ms)
    done = (n_blocks // 4) * 4
    ms = lax.cond(n_blocks - done >= 2, lambda c: sweep([done, done + 1], c), lambda c: c, ms)

    @pl.when(lax.rem(n_blocks, 2) == 1)
    def _():
        sweep([i], ms)


def _moba_kernel(slopes_ref, q_ref, k_ref, vt_ref, o_ref, kmean_ref, alibi_ref, bias_ref, acc_ref,
                 *, group):
    n_blocks = kmean_ref.shape[0]
    seq = q_ref.shape[0]
    hg = pl.program_id(1)
    i = pl.program_id(2)

    @pl.when(i == 0)
    def _():
        slopes2 = [slopes_ref[hg * group + g] * LOG2E for g in range(group)]
        for j in range(n_blocks):
            kj = k_ref[j * ATTN_BLOCK:(j + 1) * ATTN_BLOCK, :].astype(F32)
            kmean_ref[j:j + 1, :] = jnp.sum(kj, axis=0, keepdims=True) * (1.0 / ATTN_BLOCK)
        _store_alibi_tiles(alibi_ref, slopes2)

        row = lax.broadcasted_iota(jnp.int32, (n_blocks, seq), 0)
        own = lax.broadcasted_iota(jnp.int32, (n_blocks, seq), 1) // ATTN_BLOCK
        for g in range(group):
            cols = slice(g * HEAD_DIM, (g + 1) * HEAD_DIM)
            terms, rest = [], kmean_ref[:, cols]
            for _ in range(3):
                terms.append(rest.astype(BF16).astype(F32))
                rest = rest - terms[-1]
            terms.append(jnp.zeros((HEAD_DIM - 3 * n_blocks, HEAD_DIM), F32))
            stacked = jnp.concatenate(terms, axis=0).astype(BF16)
            parts = lax.dot_general(q_ref[:, cols], stacked, _NT_DIMS,
                                    preferred_element_type=F32).T
            gate = (parts[:n_blocks] + parts[n_blocks:2 * n_blocks]
                    + parts[2 * n_blocks:3 * n_blocks])
            rank = jnp.zeros(gate.shape, jnp.int32)
            for jp in range(n_blocks):
                g_jp = gate[jp:jp + 1, :]
                beats = ((g_jp > gate) | ((g_jp == gate) & (jp < row))) & (jp < own)
                rank = rank + jnp.where(beats, 1, 0)
            selected = (row < own) & (rank < MOBA_TOPK)
            block_off = slopes2[g] * ((row - own) * ATTN_BLOCK).astype(F32)
            bias = jnp.where(selected | (row == own), block_off, NEG)
            for qi in range(n_blocks):
                bias_ref[g, qi] = bias[:, qi * ATTN_BLOCK:(qi + 1) * ATTN_BLOCK]

    q_rows = pl.ds(pl.multiple_of(i * ATTN_BLOCK, ATTN_BLOCK), ATTN_BLOCK)
    qs = [q_ref[q_rows, g * HEAD_DIM:(g + 1) * HEAD_DIM] for g in range(group)]

    _flash_blocks(
        i, group,
        scores=lambda j, g: lax.dot_general(_key_block(k_ref, j, g), qs[g], _NT_DIMS,
                                            preferred_element_type=F32),
        values=lambda j, g: vt_ref[j, g],
        alibi=lambda g, own: alibi_ref[own, g],
        bias=lambda j, g: bias_ref[g, i, pl.ds(j, 1), :],
        acc_ref=acc_ref)
    for g in range(group):
        o = acc_ref[g, :HEAD_DIM, :] * (1.0 / acc_ref[g, HEAD_DIM:HEAD_DIM + 1, :])
        o_ref[:, g * HEAD_DIM:(g + 1) * HEAD_DIM] = o.T.astype(o_ref.dtype)


def _moba_attention(qk, vt, slopes, *, batch, seq, heads, q_col, k_col):
    nb = seq // ATTN_BLOCK
    t = batch * seq
    group = math.gcd(heads, MOBA_HEAD_GROUP)
    gw = group * HEAD_DIM
    v_rows = HEAD_DIM + SUM_ROWS
    assert q_col % group == 0 and k_col % group == 0 and vt.shape[1:] == (heads, v_rows, ATTN_BLOCK)
    blocks = (_nbytes((ATTN_BLOCK, gw), BF16) + 2 * _nbytes((seq, gw), BF16)
              + _nbytes((nb, group, v_rows, ATTN_BLOCK), BF16))
    scratch_types = [((nb, gw), F32), ((2, group, ATTN_BLOCK, ATTN_BLOCK), F32),
                     ((group, nb, nb, ATTN_BLOCK), F32), ((group, v_rows, ATTN_BLOCK), F32)]
    scratch = [pltpu.VMEM(shape, dtype) for shape, dtype in scratch_types]
    scratch_bytes = sum(_nbytes(shape, dtype) for shape, dtype in scratch_types)
    return pl.pallas_call(
        functools.partial(_moba_kernel, group=group),
        grid=(batch, heads // group, nb),
        in_specs=[pl.BlockSpec(memory_space=pltpu.SMEM),
                  pl.BlockSpec((seq, gw), lambda b, h, i: (b, q_col // group + h)),
                  pl.BlockSpec((seq, gw), lambda b, h, i: (b, k_col // group + h)),
                  pl.BlockSpec((nb, group, v_rows, ATTN_BLOCK), lambda b, h, i: (b, h, 0, 0))],
        out_specs=pl.BlockSpec((ATTN_BLOCK, gw), lambda b, h, i: (b * nb + i, h)),
        out_shape=jax.ShapeDtypeStruct((t, heads * HEAD_DIM), BF16),
        scratch_shapes=scratch,
        compiler_params=pltpu.CompilerParams(
            dimension_semantics=("parallel", "parallel", "arbitrary"),
            vmem_limit_bytes=_vmem_limit(blocks, scratch_bytes)),
        name="moba_attention",
    )(slopes, qk, qk, vt)


def _diff_kernel(slopes_ref, lamv_ref, g_ref, q_ref, k_ref, vt_ref, o_ref, alibi_ref, acc_ref,
                 *, lam_init, group):
    hg = pl.program_id(1)
    i = pl.program_id(2)
    slopes2 = [slopes_ref[hg * group + g] * LOG2E for g in range(group)]

    @pl.when(i == 0)
    def _():
        _store_alibi_tiles(alibi_ref, slopes2)

    qs = [q_ref[:, c * HEAD_DIM:(c + 1) * HEAD_DIM] for c in range(2 * group)]

    _flash_blocks(
        i, 2 * group,
        scores=lambda j, c: lax.dot_general(_key_block(k_ref, j, c), qs[c], _NT_DIMS,
                                            preferred_element_type=F32),
        values=lambda j, c: vt_ref[j, c // 2],
        alibi=lambda c, own: alibi_ref[own, c // 2],
        bias=lambda j, c: slopes2[c // 2] * ((j - i) * ATTN_BLOCK).astype(F32),
        acc_ref=acc_ref)

    lv = lamv_ref[...]
    lam = (jnp.exp(jnp.sum(lv[0:1] * lv[1:2], axis=-1, keepdims=True))
           - jnp.exp(jnp.sum(lv[2:3] * lv[3:4], axis=-1, keepdims=True)) + lam_init)

    def normalised(c):
        return acc_ref[c, :DIFF_V_DIM, :] * (1.0 / acc_ref[c, DIFF_V_DIM:DIFF_V_DIM + 1, :])

    gain = g_ref[...] * (1.0 - lam_init)
    for g in range(group):
        o = normalised(2 * g) - lam * normalised(2 * g + 1)
        o = o * lax.rsqrt(jnp.mean(o * o, axis=0, keepdims=True) + EPS)
        o_ref[:, g * DIFF_V_DIM:(g + 1) * DIFF_V_DIM] = (o.T * gain).astype(o_ref.dtype)


def _diff_attention(qk, vt, slopes, lam_vecs, g_subln, *, batch, seq, heads, q_col, k_col,
                    lam_init):
    nb = seq // ATTN_BLOCK
    t = batch * seq
    group = math.gcd(heads, DIFF_HEAD_GROUP)
    gw = group * DIFF_V_DIM
    v_rows = DIFF_V_DIM + SUM_ROWS
    assert q_col % group == 0 and k_col % group == 0 and vt.shape[1:] == (heads, v_rows, ATTN_BLOCK)
    blocks = (_nbytes((ATTN_BLOCK, gw), BF16) + _nbytes((seq, gw), BF16)
              + _nbytes((nb, group, v_rows, ATTN_BLOCK), BF16) + _nbytes((ATTN_BLOCK, gw), BF16))
    scratch_types = [((2, group, ATTN_BLOCK, ATTN_BLOCK), F32),
                     ((2 * group, v_rows, ATTN_BLOCK), F32)]
    scratch_bytes = sum(_nbytes(shape, dtype) for shape, dtype in scratch_types)
    return pl.pallas_call(
        functools.partial(_diff_kernel, lam_init=lam_init, group=group),
        grid=(batch, heads // group, nb),
        in_specs=[pl.BlockSpec(memory_space=pltpu.SMEM),
                  pl.BlockSpec((4, HEAD_DIM), lambda b, h, i: (0, 0)),
                  pl.BlockSpec((1, DIFF_V_DIM), lambda b, h, i: (0, 0)),
                  pl.BlockSpec((ATTN_BLOCK, gw), lambda b, h, i: (b * nb + i, q_col // group + h)),
                  pl.BlockSpec((seq, gw), lambda b, h, i: (b, k_col // group + h)),
                  pl.BlockSpec((nb, group, v_rows, ATTN_BLOCK), lambda b, h, i: (b, h, 0, 0))],
        out_specs=pl.BlockSpec((ATTN_BLOCK, gw), lambda b, h, i: (b * nb + i, h)),
        out_shape=jax.ShapeDtypeStruct((t, heads * DIFF_V_DIM), BF16),
        scratch_shapes=[pltpu.VMEM(shape, dtype) for shape, dtype in scratch_types],
        compiler_params=pltpu.CompilerParams(
            dimension_semantics=("parallel", "parallel", "arbitrary"),
            vmem_limit_bytes=_vmem_limit(blocks, scratch_bytes)),
        name="diff_attention",
    )(slopes, lam_vecs, g_subln.reshape(1, DIFF_V_DIM), qk, qk, vt)


def _alibi_slopes(n_heads):
    return 2.0 ** (-8.0 * jnp.arange(1, n_heads + 1, dtype=F32) / n_heads)


def _tile(n, want):
    if n <= want:
        return n
    t = want
    while n % t:
        t -= ATTN_BLOCK
    assert t > 0
    return t


def kernel(x, w_in, w_proj_a, w_proj_b, w_out, w_up, w_down, g_pre_mix, g_post_mix, g_pre_mlp,
           g_post_mlp, g_subln, lam_q1, lam_k1, lam_q2, lam_k2):
    batch, seq, d = x.shape
    depth = w_in.shape[0]
    t = batch * seq
    mw = d // 2
    heads_a = mw // HEAD_DIM
    heads_b = mw // DIFF_V_DIM
    assert seq % ATTN_BLOCK == 0 and mw % DIFF_V_DIM == 0
    assert w_in.shape[2] == 6 * mw + 2 * d

    slopes_a = _alibi_slopes(heads_a)
    slopes_b = _alibi_slopes(heads_b)
    tm = _tile(t, 1024)
    tn = _tile(mw, 1024)
    tm_gated = _tile(t, 512)
    wt = mw // tn
    qk_tile = lambda j: j + jnp.where(j >= 2 * wt, wt, 0)
    gate_tile = lambda j: 6 * wt + j
    plain_tile = lambda j: j

    q_scale = HEAD_DIM ** -0.5 * LOG2E

    def scale_queries(r, tile):
        return r * jnp.where((tile // wt) % 2 == 0, q_scale, 1.0)

    xf = x.reshape(t, d)
    h = _rmsnorm_cast(xf, g_pre_mix[0])
    for l in range(depth):
        qk = _proj(h, w_in, l, qk_tile, 4 * wt, out_dtype=BF16, tm=tm, tn=tn,
                   epilogue=scale_queries, name="qk_proj")
        vt_a = _proj(h, w_in, l, lambda j: 2 * wt + j, wt, out_dtype=BF16, tm=tm, tn=tn,
                     value_rows=HEAD_DIM, name="moba_value_proj")
        vt_b = _proj(h, w_in, l, lambda j: 5 * wt + j, wt, out_dtype=BF16, tm=tm, tn=tn,
                     value_rows=DIFF_V_DIM, name="diff_value_proj")
        gates = _proj(h, w_in, l, gate_tile, 2 * d // tn, out_dtype=F32, tm=tm, tn=tn,
                      name="gate_proj")

        o_a = _moba_attention(qk, vt_a, slopes_a, batch=batch, seq=seq, heads=heads_a,
                              q_col=0, k_col=heads_a)
        lam_init = 0.8 - 0.6 * math.exp(-0.3 * l)
        lam_vecs = jnp.stack([lam_q1[l], lam_k1[l], lam_q2[l], lam_k2[l]]).astype(F32)
        o_b = _diff_attention(qk, vt_b, slopes_b, lam_vecs, g_subln[l], batch=batch, seq=seq,
                              heads=heads_b, q_col=2 * heads_b, k_col=3 * heads_b,
                              lam_init=lam_init)

        y = _gated_proj(o_a, o_b, w_proj_a, w_proj_b, l, gates, tm=tm_gated, tn=tn)
        mix = _proj(y, w_out, l, plain_tile, d // tn, out_dtype=F32, tm=tm, tn=tn, name="out_proj")
        xf, h = _residual(xf, mix, g_post_mix[l], g_pre_mlp[l])

        u, w_down_bf = _proj(h, w_up, l, plain_tile, 4 * d // tn, out_dtype=BF16, tm=tm, tn=tn,
                             epilogue=_relu_squared, also_cast=w_down, name="mlp_up")
        m = _matmul_ksplit(u, w_down_bf, out_dtype=F32, tm=tm, tn=_tile(d, 1024),
                           tk=_tile(4 * d, 4096), name="mlp_down")
        g_next = g_pre_mix[l + 1] if l + 1 < depth else None
        xf, h = _residual(xf, m, g_post_mlp[l], g_next)

    return xf.reshape(batch, seq, d)
```

```python
import functools
import math

import jax
import jax.numpy as jnp
from jax import lax
from jax.experimental import pallas as pl
from jax.experimental.pallas import tpu as pltpu

F32 = jnp.float32
BF16 = jnp.bfloat16

V7X_VMEM_BYTES = 64 * 2**20
V7X_VMEM_COMPILER_RESERVE = 12 * 2**20

HEAD_DIM = 128
DIFF_V_DIM = 256
ATTN_BLOCK = 256
MOBA_TOPK = 3
EPS = 1e-6
NEG = -1e30
LOG2E = math.log2(math.e)

MOBA_HEAD_GROUP = 8
DIFF_HEAD_GROUP = 4
SUM_ROWS = 16

_NT_DIMS = (((1,), (1,)), ((), ()))


def _vmem_limit(block_bytes, scratch_bytes=0):
    want = 2 * block_bytes + scratch_bytes + V7X_VMEM_COMPILER_RESERVE
    return int(min(want, V7X_VMEM_BYTES - 2 * 2**20))


def _nbytes(shape, dtype):
    return math.prod(shape) * jnp.dtype(dtype).itemsize


def _rms(x):
    return x * lax.rsqrt(jnp.mean(x * x, axis=-1, keepdims=True) + EPS)


def _rmsnorm_cast_kernel(x_ref, g_ref, h_ref):
    h_ref[...] = (_rms(x_ref[...]) * g_ref[...]).astype(h_ref.dtype)


def _rmsnorm_cast(x, g, *, rows=256):
    t, d = x.shape
    assert t % rows == 0
    blocks = _nbytes((rows, d), F32) + _nbytes((rows, d), BF16)
    return pl.pallas_call(
        _rmsnorm_cast_kernel,
        grid=(t // rows,),
        in_specs=[pl.BlockSpec((rows, d), lambda i: (i, 0)),
                  pl.BlockSpec((1, d), lambda i: (0, 0))],
        out_specs=pl.BlockSpec((rows, d), lambda i: (i, 0)),
        out_shape=jax.ShapeDtypeStruct((t, d), BF16),
        compiler_params=pltpu.CompilerParams(
            dimension_semantics=("parallel",), vmem_limit_bytes=_vmem_limit(blocks)),
        name="rmsnorm_cast",
    )(x, g.reshape(1, d))


def _residual_kernel(x_ref, m_ref, gpost_ref, gnext_ref, xo_ref, h_ref):
    xn = x_ref[...] + _rms(m_ref[...]) * gpost_ref[...]
    xo_ref[...] = xn
    h_ref[...] = (_rms(xn) * gnext_ref[...]).astype(h_ref.dtype)


def _residual_last_kernel(x_ref, m_ref, gpost_ref, xo_ref):
    xo_ref[...] = x_ref[...] + _rms(m_ref[...]) * gpost_ref[...]


def _residual(x, m, g_post, g_next, *, rows=256):
    t, d = x.shape
    assert t % rows == 0
    row_spec = pl.BlockSpec((rows, d), lambda i: (i, 0))
    gain_spec = pl.BlockSpec((1, d), lambda i: (0, 0))
    blocks = 3 * _nbytes((rows, d), F32) + _nbytes((rows, d), BF16)
    params = pltpu.CompilerParams(
        dimension_semantics=("parallel",), vmem_limit_bytes=_vmem_limit(blocks))
    if g_next is None:
        return pl.pallas_call(
            _residual_last_kernel,
            grid=(t // rows,),
            in_specs=[row_spec, row_spec, gain_spec],
            out_specs=row_spec,
            out_shape=jax.ShapeDtypeStruct((t, d), F32),
            compiler_params=params,
            name="residual_last",
        )(x, m, g_post.reshape(1, d)), None
    return pl.pallas_call(
        _residual_kernel,
        grid=(t // rows,),
        in_specs=[row_spec, row_spec, gain_spec, gain_spec],
        out_specs=[row_spec, row_spec],
        out_shape=[jax.ShapeDtypeStruct((t, d), F32), jax.ShapeDtypeStruct((t, d), BF16)],
        compiler_params=params,
        name="residual_norm",
    )(x, m, g_post.reshape(1, d), g_next.reshape(1, d))


def _identity(x, tile):
    return x


def _relu_squared(x, tile):
    r = jnp.maximum(x, 0.0)
    return r * r


def _proj_kernel(x_ref, w_hbm, o_ref, wstage_ref, wbf_ref, sem, *, layer, col_tile, n_tiles,
                 epilogue, key_blocked_t):
    j = pl.program_id(0)
    k, tn = wstage_ref.shape

    def weight_copy(tile):
        col = pl.multiple_of(col_tile(tile) * tn, tn)
        return pltpu.make_async_copy(w_hbm.at[layer, :, pl.ds(col, tn)], wstage_ref, sem)

    def product(round_weights):
        slabs = [slice(c * tn, (c + 1) * tn) for c in range(k // tn)] if round_weights else [
            slice(None)]
        acc = None
        for rows in slabs:
            if key_blocked_t:
                if round_weights:
                    wbf_ref[:, rows] = wstage_ref[rows, :].T.astype(BF16)
                part = lax.dot_general(wbf_ref[:, rows], x_ref[:, rows], _NT_DIMS,
                                       preferred_element_type=F32)
            else:
                if round_weights:
                    wbf_ref[rows, :] = wstage_ref[rows, :].astype(BF16)
                part = jnp.dot(x_ref[:, rows], wbf_ref[rows, :], preferred_element_type=F32)
            acc = part if acc is None else acc + part
        return acc

    def store(r):
        r = epilogue(r, j)
        if key_blocked_t:
            n_chunks, n_heads, rows, _ = o_ref.shape
            rows -= SUM_ROWS
            ones = jnp.ones((SUM_ROWS, ATTN_BLOCK), o_ref.dtype)
            for c in range(n_chunks):
                for hh in range(n_heads):
                    o_ref[c, hh, :rows, :] = r[hh * rows:(hh + 1) * rows,
                                               c * ATTN_BLOCK:(c + 1) * ATTN_BLOCK].astype(o_ref.dtype)
                    o_ref[c, hh, rows:, :] = ones
        else:
            o_ref[...] = r.astype(o_ref.dtype)

    first_token_step = pl.program_id(1) == 0

    @pl.when(first_token_step)
    def _():
        @pl.when(j == 0)
        def _():
            weight_copy(j).start()

        weight_copy(j).wait()
        r = product(round_weights=True)

        @pl.when(j + 1 < n_tiles)
        def _():
            weight_copy(j + 1).start()

        store(r)

    @pl.when(jnp.logical_not(first_token_step))
    def _():
        store(product(round_weights=False))


def _proj_and_cast_kernel(x_ref, w_hbm, side_ref, o_ref, side_bf_ref, *scratch, **static):
    side_bf_ref[...] = side_ref[...].astype(side_bf_ref.dtype)
    _proj_kernel(x_ref, w_hbm, o_ref, *scratch, **static)


def _proj(x, w, layer, col_tile, n_tiles, *, out_dtype, tm, tn, epilogue=_identity,
          value_rows=None, also_cast=None, name):
    m, k = x.shape
    assert m % tm == 0 and w.shape[1] == k and w.shape[2] % tn == 0 and k % tn == 0
    n_m = m // tm
    key_blocked_t = value_rows is not None
    if key_blocked_t:
        assert tm % ATTN_BLOCK == 0 and tn % value_rows == 0
        hpt = tn // value_rows
        block = (tm // ATTN_BLOCK, hpt, value_rows + SUM_ROWS, ATTN_BLOCK)
        out_shape = jax.ShapeDtypeStruct((m // ATTN_BLOCK, n_tiles * hpt) + block[2:], out_dtype)
        out_spec = pl.BlockSpec(block, lambda j, i: (i, j, 0, 0))
    else:
        block = (tm, tn)
        out_shape = jax.ShapeDtypeStruct((m, n_tiles * tn), out_dtype)
        out_spec = pl.BlockSpec(block, lambda j, i: (i, j))
    blocks = _nbytes((tm, k), BF16) + _nbytes(block, out_dtype)
    scratch_bytes = _nbytes((k, tn), F32) + _nbytes((k, tn), BF16)
    body = _proj_kernel
    operands = [x, w]
    in_specs = [pl.BlockSpec((tm, k), lambda j, i: (i, 0)), pl.BlockSpec(memory_space=pl.ANY)]
    if also_cast is not None:
        _, side_k, side_n = also_cast.shape
        assert side_k % (n_tiles * n_m) == 0
        slab = (side_k // (n_tiles * n_m), side_n)
        body = _proj_and_cast_kernel
        operands.append(also_cast)
        in_specs.append(pl.BlockSpec((None,) + slab, lambda j, i: (layer, j * n_m + i, 0)))
        out_spec = [out_spec, pl.BlockSpec(slab, lambda j, i: (j * n_m + i, 0))]
        out_shape = [out_shape, jax.ShapeDtypeStruct((side_k, side_n), BF16)]
        blocks += _nbytes(slab, F32) + _nbytes(slab, BF16)
    return pl.pallas_call(
        functools.partial(body, layer=layer, col_tile=col_tile, n_tiles=n_tiles,
                          epilogue=epilogue, key_blocked_t=key_blocked_t),
        grid=(n_tiles, n_m),
        in_specs=in_specs,
        out_specs=out_spec,
        out_shape=out_shape,
        scratch_shapes=[pltpu.VMEM((k, tn), F32),
                        pltpu.VMEM((tn, k) if key_blocked_t else (k, tn), BF16),
                        pltpu.SemaphoreType.DMA(())],
        compiler_params=pltpu.CompilerParams(
            dimension_semantics=("arbitrary", "arbitrary"),
            vmem_limit_bytes=_vmem_limit(blocks, scratch_bytes)),
        name=name,
    )(*operands)


def _matmul_ksplit_kernel(x_ref, w_ref, o_ref, acc_ref, *, n_k):
    k = pl.program_id(2)

    @pl.when(k == 0)
    def _():
        acc_ref[...] = jnp.zeros_like(acc_ref)

    acc_ref[...] += jnp.dot(x_ref[...], w_ref[...], preferred_element_type=F32)

    @pl.when(k == n_k - 1)
    def _():
        o_ref[...] = acc_ref[...].astype(o_ref.dtype)


def _matmul_ksplit(x, w, *, out_dtype, tm, tn, tk, name):
    m, k = x.shape
    _, n = w.shape
    assert m % tm == 0 and n % tn == 0 and k % tk == 0
    n_k = k // tk
    blocks = _nbytes((tm, tk), BF16) + _nbytes((tk, tn), BF16) + _nbytes((tm, tn), out_dtype)
    return pl.pallas_call(
        functools.partial(_matmul_ksplit_kernel, n_k=n_k),
        grid=(m // tm, n // tn, n_k),
        in_specs=[pl.BlockSpec((tm, tk), lambda i, j, kk: (i, kk)),
                  pl.BlockSpec((tk, tn), lambda i, j, kk: (kk, j))],
        out_specs=pl.BlockSpec((tm, tn), lambda i, j, kk: (i, j)),
        out_shape=jax.ShapeDtypeStruct((m, n), out_dtype),
        scratch_shapes=[pltpu.VMEM((tm, tn), F32)],
        compiler_params=pltpu.CompilerParams(
            dimension_semantics=("parallel", "parallel", "arbitrary"),
            vmem_limit_bytes=_vmem_limit(blocks, _nbytes((tm, tn), F32))),
        name=name,
    )(x, w)


def _sigmoid_gates(x, tile):
    return 0.5 * jnp.tanh(0.5 * x) + 0.5


def _gated_proj_kernel(oa_ref, ob_ref, wa_hbm, wb_hbm, ga_ref, gb_ref, y_ref, wa_stage, wb_stage,
                       wabf_ref, wbbf_ref, sem, *, layer, n_tiles):
    j = pl.program_id(0)
    tn = wa_stage.shape[1]

    def weight_copies(tile):
        col = pl.multiple_of(tile * tn, tn)
        return [pltpu.make_async_copy(w_hbm.at[layer, :, pl.ds(col, tn)], stage, sem.at[n])
                for n, (w_hbm, stage) in enumerate(((wa_hbm, wa_stage), (wb_hbm, wb_stage)))]

    def product(x_ref, stage_ref, wbf_ref, round_weights):
        k = stage_ref.shape[0]
        slabs = ([slice(c * tn, (c + 1) * tn) for c in range(k // tn)] if round_weights else
                 [slice(None)])
        acc = None
        for rows in slabs:
            if round_weights:
                wbf_ref[rows, :] = stage_ref[rows, :].astype(BF16)
            part = jnp.dot(x_ref[:, rows], wbf_ref[rows, :], preferred_element_type=F32)
            acc = part if acc is None else acc + part
        return acc

    def products(round_weights):
        return (product(oa_ref, wa_stage, wabf_ref, round_weights),
                product(ob_ref, wb_stage, wbbf_ref, round_weights))

    def merge(pa, pb):
        y = ga_ref[...] * pa + gb_ref[...] * pb
        y_ref[...] = y.astype(y_ref.dtype)

    first_token_step = pl.program_id(1) == 0

    @pl.when(first_token_step)
    def _():
        @pl.when(j == 0)
        def _():
            for copy in weight_copies(j):
                copy.start()

        for copy in weight_copies(j):
            copy.wait()
        pa, pb = products(round_weights=True)

        @pl.when(j + 1 < n_tiles)
        def _():
            for copy in weight_copies(j + 1):
                copy.start()

        merge(pa, pb)

    @pl.when(jnp.logical_not(first_token_step))
    def _():
        merge(*products(round_weights=False))


def _gated_proj(o_a, o_b, w_a, w_b, layer, gates, *, tm, tn):
    t, ka = o_a.shape
    _, kb = o_b.shape
    d = w_a.shape[2]
    assert t % tm == 0 and d % tn == 0 and ka % tn == 0 and kb % tn == 0
    n_n = d // tn
    blocks = (_nbytes((tm, ka), BF16) + _nbytes((tm, kb), BF16) + 2 * _nbytes((tm, tn), F32)
              + _nbytes((tm, tn), BF16))
    scratch_types = [((ka, tn), F32), ((kb, tn), F32), ((ka, tn), BF16), ((kb, tn), BF16)]
    scratch_bytes = sum(_nbytes(shape, dtype) for shape, dtype in scratch_types)
    return pl.pallas_call(
        functools.partial(_gated_proj_kernel, layer=layer, n_tiles=n_n),
        grid=(n_n, t // tm),
        in_specs=[pl.BlockSpec((tm, ka), lambda j, i: (i, 0)),
                  pl.BlockSpec((tm, kb), lambda j, i: (i, 0)),
                  pl.BlockSpec(memory_space=pl.ANY),
                  pl.BlockSpec(memory_space=pl.ANY),
                  pl.BlockSpec((tm, tn), lambda j, i: (i, j)),
                  pl.BlockSpec((tm, tn), lambda j, i: (i, n_n + j))],
        out_specs=pl.BlockSpec((tm, tn), lambda j, i: (i, j)),
        out_shape=jax.ShapeDtypeStruct((t, d), BF16),
        scratch_shapes=[pltpu.VMEM(shape, dtype) for shape, dtype in scratch_types]
        + [pltpu.SemaphoreType.DMA((2,))],
        compiler_params=pltpu.CompilerParams(
            dimension_semantics=("arbitrary", "arbitrary"),
            vmem_limit_bytes=_vmem_limit(blocks, scratch_bytes)),
        name="gated_proj",
    )(o_a, o_b, w_a, w_b, gates, gates)


def _tile_indices():
    kk = lax.broadcasted_iota(jnp.int32, (ATTN_BLOCK, ATTN_BLOCK), 0)
    qq = lax.broadcasted_iota(jnp.int32, (ATTN_BLOCK, ATTN_BLOCK), 1)
    return kk, qq


def _key_block(k_ref, j, g):
    rows = pl.ds(pl.multiple_of(j * ATTN_BLOCK, ATTN_BLOCK), ATTN_BLOCK)
    return k_ref[rows, g * HEAD_DIM:(g + 1) * HEAD_DIM]


def _store_alibi_tiles(alibi_ref, slopes2):
    kk, qq = _tile_indices()
    dist = (kk - qq).astype(F32)
    for g, slope2 in enumerate(slopes2):
        alibi_ref[0, g] = slope2 * dist
        alibi_ref[1, g] = jnp.where(kk <= qq, slope2 * dist, NEG)


def _flash_blocks(i, n_units, *, scores, values, alibi, bias, acc_ref):
    units = range(n_units)

    def sweep(rs, ms):
        ms = list(ms)
        blocks = [jnp.where(r == 0, i, r - 1) for r in rs]
        owns = [jnp.where(r == 0, 1, 0) for r in rs]
        ss = [[scores(j, u) for u in units] for j in blocks]
        ps, alphas = [], []
        for n, j in enumerate(blocks):
            ps.append([])
            alphas.append([])
            for u in units:
                t = ss[n][u] + alibi(u, owns[n])
                b = bias(j, u)
                m_new = jnp.maximum(ms[u], jnp.max(t, axis=0, keepdims=True) + b)
                ps[n].append(jnp.exp2(t - (m_new - b)).astype(BF16))
                alphas[n].append(jnp.exp2(ms[u] - m_new))
                ms[u] = m_new
        for u in units:
            acc = acc_ref[u]
            for n, j in enumerate(blocks):
                acc = alphas[n][u] * acc + jnp.dot(values(j, u), ps[n][u],
                                                   preferred_element_type=F32)
            acc_ref[u] = acc
        return tuple(ms)

    for u in units:
        acc_ref[u] = jnp.zeros(acc_ref.shape[1:], acc_ref.dtype)
    ms = tuple(jnp.full((1, ATTN_BLOCK), NEG, F32) for _ in units)

    n_blocks = i + 1
    ms = lax.fori_loop(0, n_blocks // 4,
                       lambda k, c: sweep([4 * k, 4 * k + 1, 4 * k + 2, 4 * k + 3], c), ms)
    done = (n_blocks // 4) * 4
    ms = lax.cond(n_blocks - done >= 2, lambda c: sweep([done, done + 1], c), lambda c: c, ms)

    @pl.when(lax.rem(n_blocks, 2) == 1)
    def _():
        sweep([i], ms)


def _moba_kernel(slopes_ref, q_ref, k_ref, vt_ref, o_ref, kmean_ref, alibi_ref, bias_ref, acc_ref,
                 *, group):
    n_blocks = kmean_ref.shape[0]
    seq = q_ref.shape[0]
    hg = pl.program_id(1)
    i = pl.program_id(2)

    @pl.when(i == 0)
    def _():
        slopes2 = [slopes_ref[hg * group + g] * LOG2E for g in range(group)]
        for j in range(n_blocks):
            kj = k_ref[j * ATTN_BLOCK:(j + 1) * ATTN_BLOCK, :].astype(F32)
            kmean_ref[j:j + 1, :] = jnp.sum(kj, axis=0, keepdims=True) * (1.0 / ATTN_BLOCK)
        _store_alibi_tiles(alibi_ref, slopes2)

        row = lax.broadcasted_iota(jnp.int32, (n_blocks, seq), 0)
        own = lax.broadcasted_iota(jnp.int32, (n_blocks, seq), 1) // ATTN_BLOCK
        for g in range(group):
            cols = slice(g * HEAD_DIM, (g + 1) * HEAD_DIM)
            terms, rest = [], kmean_ref[:, cols]
            for _ in range(3):
                terms.append(rest.astype(BF16).astype(F32))
                rest = rest - terms[-1]
            terms.append(jnp.zeros((HEAD_DIM - 3 * n_blocks, HEAD_DIM), F32))
            stacked = jnp.concatenate(terms, axis=0).astype(BF16)
            parts = lax.dot_general(q_ref[:, cols], stacked, _NT_DIMS,
                                    preferred_element_type=F32).T
            gate = (parts[:n_blocks] + parts[n_blocks:2 * n_blocks]
                    + parts[2 * n_blocks:3 * n_blocks])
            rank = jnp.zeros(gate.shape, jnp.int32)
            for jp in range(n_blocks):
                g_jp = gate[jp:jp + 1, :]
                beats = ((g_jp > gate) | ((g_jp == gate) & (jp < row))) & (jp < own)
                rank = rank + jnp.where(beats, 1, 0)
            selected = (row < own) & (rank < MOBA_TOPK)
            block_off = slopes2[g] * ((row - own) * ATTN_BLOCK).astype(F32)
            bias = jnp.where(selected | (row == own), block_off, NEG)
            for qi in range(n_blocks):
                bias_ref[g, qi] = bias[:, qi * ATTN_BLOCK:(qi + 1) * ATTN_BLOCK]

    q_rows = pl.ds(pl.multiple_of(i * ATTN_BLOCK, ATTN_BLOCK), ATTN_BLOCK)
    qs = [q_ref[q_rows, g * HEAD_DIM:(g + 1) * HEAD_DIM] for g in range(group)]

    _flash_blocks(
        i, group,
        scores=lambda j, g: lax.dot_general(_key_block(k_ref, j, g), qs[g], _NT_DIMS,
                                            preferred_element_type=F32),
        values=lambda j, g: vt_ref[j, g],
        alibi=lambda g, own: alibi_ref[own, g],
        bias=lambda j, g: bias_ref[g, i, pl.ds(j, 1), :],
        acc_ref=acc_ref)
    for g in range(group):
        o = acc_ref[g, :HEAD_DIM, :] * (1.0 / acc_ref[g, HEAD_DIM:HEAD_DIM + 1, :])
        o_ref[:, g * HEAD_DIM:(g + 1) * HEAD_DIM] = o.T.astype(o_ref.dtype)


def _moba_attention(qk, vt, slopes, *, batch, seq, heads, q_col, k_col):
    nb = seq // ATTN_BLOCK
    t = batch * seq
    group = math.gcd(heads, MOBA_HEAD_GROUP)
    gw = group * HEAD_DIM
    v_rows = HEAD_DIM + SUM_ROWS
    assert q_col % group == 0 and k_col % group == 0 and vt.shape[1:] == (heads, v_rows, ATTN_BLOCK)
    blocks = (_nbytes((ATTN_BLOCK, gw), BF16) + 2 * _nbytes((seq, gw), BF16)
              + _nbytes((nb, group, v_rows, ATTN_BLOCK), BF16))
    scratch_types = [((nb, gw), F32), ((2, group, ATTN_BLOCK, ATTN_BLOCK), F32),
                     ((group, nb, nb, ATTN_BLOCK), F32), ((group, v_rows, ATTN_BLOCK), F32)]
    scratch = [pltpu.VMEM(shape, dtype) for shape, dtype in scratch_types]
    scratch_bytes = sum(_nbytes(shape, dtype) for shape, dtype in scratch_types)
    return pl.pallas_call(
        functools.partial(_moba_kernel, group=group),
        grid=(batch, heads // group, nb),
        in_specs=[pl.BlockSpec(memory_space=pltpu.SMEM),
                  pl.BlockSpec((seq, gw), lambda b, h, i: (b, q_col // group + h)),
                  pl.BlockSpec((seq, gw), lambda b, h, i: (b, k_col // group + h)),
                  pl.BlockSpec((nb, group, v_rows, ATTN_BLOCK), lambda b, h, i: (b, h, 0, 0))],
        out_specs=pl.BlockSpec((ATTN_BLOCK, gw), lambda b, h, i: (b * nb + i, h)),
        out_shape=jax.ShapeDtypeStruct((t, heads * HEAD_DIM), BF16),
        scratch_shapes=scratch,
        compiler_params=pltpu.CompilerParams(
            dimension_semantics=("parallel", "parallel", "arbitrary"),
            vmem_limit_bytes=_vmem_limit(blocks, scratch_bytes)),
        name="moba_attention",
    )(slopes, qk, qk, vt)


def _diff_kernel(slopes_ref, lamv_ref, g_ref, q_ref, k_ref, vt_ref, o_ref, alibi_ref, acc_ref,
                 *, lam_init, group):
    hg = pl.program_id(1)
    i = pl.program_id(2)
    slopes2 = [slopes_ref[hg * group + g] * LOG2E for g in range(group)]

    @pl.when(i == 0)
    def _():
        _store_alibi_tiles(alibi_ref, slopes2)

    qs = [q_ref[:, c * HEAD_DIM:(c + 1) * HEAD_DIM] for c in range(2 * group)]

    _flash_blocks(
        i, 2 * group,
        scores=lambda j, c: lax.dot_general(_key_block(k_ref, j, c), qs[c], _NT_DIMS,
                                            preferred_element_type=F32),
        values=lambda j, c: vt_ref[j, c // 2],
        alibi=lambda c, own: alibi_ref[own, c // 2],
        bias=lambda j, c: slopes2[c // 2] * ((j - i) * ATTN_BLOCK).astype(F32),
        acc_ref=acc_ref)

    lv = lamv_ref[...]
    lam = (jnp.exp(jnp.sum(lv[0:1] * lv[1:2], axis=-1, keepdims=True))
           - jnp.exp(jnp.sum(lv[2:3] * lv[3:4], axis=-1, keepdims=True)) + lam_init)

    def normalised(c):
        return acc_ref[c, :DIFF_V_DIM, :] * (1.0 / acc_ref[c, DIFF_V_DIM:DIFF_V_DIM + 1, :])

    gain = g_ref[...] * (1.0 - lam_init)
    for g in range(group):
        o = normalised(2 * g) - lam * normalised(2 * g + 1)
        o = o * lax.rsqrt(jnp.mean(o * o, axis=0, keepdims=True) + EPS)
        o_ref[:, g * DIFF_V_DIM:(g + 1) * DIFF_V_DIM] = (o.T * gain).astype(o_ref.dtype)


def _diff_attention(qk, vt, slopes, lam_vecs, g_subln, *, batch, seq, heads, q_col, k_col,
                    lam_init):
    nb = seq // ATTN_BLOCK
    t = batch * seq
    group = math.gcd(heads, DIFF_HEAD_GROUP)
    gw = group * DIFF_V_DIM
    v_rows = DIFF_V_DIM + SUM_ROWS
    assert q_col % group == 0 and k_col % group == 0 and vt.shape[1:] == (heads, v_rows, ATTN_BLOCK)
    blocks = (_nbytes((ATTN_BLOCK, gw), BF16) + _nbytes((seq, gw), BF16)
              + _nbytes((nb, group, v_rows, ATTN_BLOCK), BF16) + _nbytes((ATTN_BLOCK, gw), BF16))
    scratch_types = [((2, group, ATTN_BLOCK, ATTN_BLOCK), F32),
                     ((2 * group, v_rows, ATTN_BLOCK), F32)]
    scratch_bytes = sum(_nbytes(shape, dtype) for shape, dtype in scratch_types)
    return pl.pallas_call(
        functools.partial(_diff_kernel, lam_init=lam_init, group=group),
        grid=(batch, heads // group, nb),
        in_specs=[pl.BlockSpec(memory_space=pltpu.SMEM),
                  pl.BlockSpec((4, HEAD_DIM), lambda b, h, i: (0, 0)),
                  pl.BlockSpec((1, DIFF_V_DIM), lambda b, h, i: (0, 0)),
                  pl.BlockSpec((ATTN_BLOCK, gw), lambda b, h, i: (b * nb + i, q_col // group + h)),
                  pl.BlockSpec((seq, gw), lambda b, h, i: (b, k_col // group + h)),
                  pl.BlockSpec((nb, group, v_rows, ATTN_BLOCK), lambda b, h, i: (b, h, 0, 0))],
        out_specs=pl.BlockSpec((ATTN_BLOCK, gw), lambda b, h, i: (b * nb + i, h)),
        out_shape=jax.ShapeDtypeStruct((t, heads * DIFF_V_DIM), BF16),
        scratch_shapes=[pltpu.VMEM(shape, dtype) for shape, dtype in scratch_types],
        compiler_params=pltpu.CompilerParams(
            dimension_semantics=("parallel", "parallel", "arbitrary"),
            vmem_limit_bytes=_vmem_limit(blocks, scratch_bytes)),
        name="diff_attention",
    )(slopes, lam_vecs, g_subln.reshape(1, DIFF_V_DIM), qk, qk, vt)


def _alibi_slopes(n_heads):
    return 2.0 ** (-8.0 * jnp.arange(1, n_heads + 1, dtype=F32) / n_heads)


def _tile(n, want):
    if n <= want:
        return n
    t = want
    while n % t:
        t -= ATTN_BLOCK
    assert t > 0
    return t


def kernel(x, w_in, w_proj_a, w_proj_b, w_out, w_up, w_down, g_pre_mix, g_post_mix, g_pre_mlp,
           g_post_mlp, g_subln, lam_q1, lam_k1, lam_q2, lam_k2):
    batch, seq, d = x.shape
    depth = w_in.shape[0]
    t = batch * seq
    mw = d // 2
    heads_a = mw // HEAD_DIM
    heads_b = mw // DIFF_V_DIM
    assert seq % ATTN_BLOCK == 0 and mw % DIFF_V_DIM == 0
    assert w_in.shape[2] == 6 * mw + 2 * d

    slopes_a = _alibi_slopes(heads_a)
    slopes_b = _alibi_slopes(heads_b)
    tm = _tile(t, 1024)
    tn = _tile(mw, 1024)
    tm_gated = _tile(t, 512)
    wt = mw // tn
    qk_tile = lambda j: j + jnp.where(j >= 2 * wt, wt, 0)
    gate_tile = lambda j: 6 * wt + j
    plain_tile = lambda j: j

    q_scale = HEAD_DIM ** -0.5 * LOG2E

    def scale_queries(r, tile):
        return r * jnp.where((tile // wt) % 2 == 0, q_scale, 1.0)

    xf = x.reshape(t, d)
    h = _rmsnorm_cast(xf, g_pre_mix[0])
    for l in range(depth):
        qk = _proj(h, w_in, l, qk_tile, 4 * wt, out_dtype=BF16, tm=tm, tn=tn,
                   epilogue=scale_queries, name="qk_proj")
        vt_a = _proj(h, w_in, l, lambda j: 2 * wt + j, wt, out_dtype=BF16, tm=tm, tn=tn,
                     value_rows=HEAD_DIM, name="moba_value_proj")
        vt_b = _proj(h, w_in, l, lambda j: 5 * wt + j, wt, out_dtype=BF16, tm=tm, tn=tn,
                     value_rows=DIFF_V_DIM, name="diff_value_proj")
        gates = _proj(h, w_in, l, gate_tile, 2 * d // tn, out_dtype=F32, tm=tm, tn=tn,
                      epilogue=_sigmoid_gates, name="gate_proj")

        o_a = _moba_attention(qk, vt_a, slopes_a, batch=batch, seq=seq, heads=heads_a,
                              q_col=0, k_col=heads_a)
        lam_init = 0.8 - 0.6 * math.exp(-0.3 * l)
        lam_vecs = jnp.stack([lam_q1[l], lam_k1[l], lam_q2[l], lam_k2[l]]).astype(F32)
        o_b = _diff_attention(qk, vt_b, slopes_b, lam_vecs, g_subln[l], batch=batch, seq=seq,
                              heads=heads_b, q_col=2 * heads_b, k_col=3 * heads_b,
                              lam_init=lam_init)

        y = _gated_proj(o_a, o_b, w_proj_a, w_proj_b, l, gates, tm=tm_gated, tn=tn)
        mix = _proj(y, w_out, l, plain_tile, d // tn, out_dtype=F32, tm=tm, tn=tn, name="out_proj")
        xf, h = _residual(xf, mix, g_post_mix[l], g_pre_mlp[l])

        u, w_down_bf = _proj(h, w_up, l, plain_tile, 4 * d // tn, out_dtype=BF16, tm=tm, tn=tn,
                             epilogue=_relu_squared, also_cast=w_down, name="mlp_up")
        m = _matmul_ksplit(u, w_down_bf, out_dtype=F32, tm=tm, tn=_tile(d, 1024),
                           tk=_tile(4 * d, 4096), name="mlp_down")
        g_next = g_pre_mix[l + 1] if l + 1 < depth else None
        xf, h = _residual(xf, m, g_post_mlp[l], g_next)

    return xf.reshape(batch, seq, d)
```
